```python
import math
import jax
import jax.numpy as jnp
from jax import lax
import numpy as np

D_MODEL = 1024
BATCH = 8
SEQ = 2048
DEPTH = 4

N_MIXERS = 2
N_EVEN = (DEPTH + 1) // 2
N_ODD = DEPTH // 2
NORM_EPS = 1e-6

RWKV_HEAD_DIM = 64
RWKV_HEADS = D_MODEL // RWKV_HEAD_DIM
LORA_DECAY = 64
LORA_AAA = 64
LORA_MV = 32
LORA_GATE = 160
LNX_EPS = 64e-5
N_SHIFT_MIX = 6

S5_GROUP = 16
S5_GROUPS = D_MODEL // S5_GROUP
S5_STATE = 64
S5_DT_MIN = 1e-3
S5_DT_MAX = 1e-1
S5_EIG_MAX = -1e-4

D_FF = 2816
N_EXPERTS = 8
TOP_K = 2
D_EXPERT = 3584
MOE_BLOCK = 256

kernel_name = 'rwkv7_s5_moe_hybrid'


def rmsnorm(x, g):
    xf = x.astype(jnp.float32)
    y = xf * lax.rsqrt(jnp.mean(xf * xf, axis=-1, keepdims=True) + NORM_EPS)
    return (y * g.astype(jnp.float32)).astype(x.dtype)


def token_shift(x):
    return jnp.pad(x, ((0, 0), (1, 0), (0, 0)))[:, :-1]


def wkv7_scan(r, decay, k, v, a_vec, b_vec):
    bsz, _, nh, nd = r.shape

    def step(state, inp):
        r_t, w_t, k_t, v_t, a_t, b_t = inp
        sa = jnp.einsum('bhij,bhj->bhi', state, a_t)
        state = (state * w_t[:, :, None, :] + sa[..., None] * b_t[:, :, None, :]
                 + v_t[..., None] * k_t[:, :, None, :])
        return state, jnp.einsum('bhij,bhj->bhi', state, r_t)

    seq = tuple(jnp.moveaxis(t.astype(jnp.float32), 1, 0) for t in (r, decay, k, v, a_vec, b_vec))
    s0 = jnp.zeros((bsz, nh, nd, nd), jnp.float32)
    _, y = lax.scan(step, s0, seq)
    return jnp.moveaxis(y, 0, 1)


def rwkv7_time_mix(x, mu, w_rkv, w0, w1, w2, a0, a1, a2, g1, g2, k_k, k_a, r_k,
                   lnx_w, lnx_b, w_o, v_first, v_mix):
    bsz, seqlen, dim = x.shape
    heads = lambda t: t.reshape(bsz, seqlen, RWKV_HEADS, RWKV_HEAD_DIM)
    xx = token_shift(x) - x
    xs = x[None] + xx[None] * mu[:, None, None, :]
    r, k, v = jnp.einsum('sbtc,scd->sbtd', xs[:3], w_rkv)
    xv, xw, xa, xg = xs[2], xs[3], xs[4], xs[5]
    log_w = -jax.nn.softplus(-(w0 + jnp.tanh(xw @ w1) @ w2).astype(jnp.float32)) - 0.5
    decay = jnp.exp(-jnp.exp(log_w))
    if v_mix is None:
        v_first = v
    else:
        v0, v1, v2 = v_mix
        v = v + (v_first - v) * jax.nn.sigmoid(v0 + (xv @ v1) @ v2)
    a = jax.nn.sigmoid(a0 + (xa @ a1) @ a2)
    g = jax.nn.sigmoid(xg @ g1) @ g2
    kk = heads((k * k_k).astype(jnp.float32))
    kk = kk / jnp.maximum(jnp.linalg.norm(kk, axis=-1, keepdims=True), 1e-12)
    k = k * (1.0 + (a - 1.0) * k_a)
    rh, kh, vh, ah = heads(r), heads(k), heads(v), heads(a)
    y = wkv7_scan(rh, heads(decay), kh, vh, -kk, kk * ah)
    mean = jnp.mean(y, axis=-1, keepdims=True)
    var = jnp.mean((y - mean) ** 2, axis=-1, keepdims=True)
    y = (y - mean) * lax.rsqrt(var + LNX_EPS)
    y = (y * lnx_w.reshape(RWKV_HEADS, RWKV_HEAD_DIM).astype(jnp.float32)
         + lnx_b.reshape(RWKV_HEADS, RWKV_HEAD_DIM).astype(jnp.float32))
    bonus = jnp.sum(rh * kh * r_k, axis=-1, keepdims=True) * vh
    y = (y + bonus).reshape(bsz, seqlen, dim).astype(x.dtype)
    return (y * g) @ w_o, v_first


def _complex_scan_combine(earlier, later):
    a1r, a1i, b1r, b1i = earlier
    a2r, a2i, b2r, b2i = later
    return (a2r * a1r - a2i * a1i, a2r * a1i + a2i * a1r,
            a2r * b1r - a2i * b1i + b2r, a2r * b1i + a2i * b1r + b2i)


def s5_mixer(x, lam_re, lam_im, log_step, b_re, b_im, c_re, c_im, d_skip, w_glu, b_glu):
    bsz, seqlen, dim = x.shape
    f32 = jnp.float32
    u = x.astype(f32).reshape(bsz, seqlen, S5_GROUPS, S5_GROUP)
    lr = jnp.minimum(lam_re.astype(f32), S5_EIG_MAX)
    li = lam_im.astype(f32)
    dt = jnp.exp(log_step.astype(f32))[:, None]
    dr, di = lr * dt, li * dt
    abar_re = jnp.exp(dr) * jnp.cos(di)
    abar_im = jnp.exp(dr) * jnp.sin(di)
    num_re = jnp.expm1(dr) * jnp.cos(di) - 2.0 * jnp.sin(0.5 * di) ** 2
    den = lr * lr + li * li
    coef_re = (num_re * lr + abar_im * li) / den
    coef_im = (abar_im * lr - num_re * li) / den
    bre, bim = b_re.astype(f32), b_im.astype(f32)
    bbar_re = coef_re[..., None] * bre - coef_im[..., None] * bim
    bbar_im = coef_re[..., None] * bim + coef_im[..., None] * bre
    bu_re = jnp.einsum('btgc,gpc->btgp', u, bbar_re)
    bu_im = jnp.einsum('btgc,gpc->btgp', u, bbar_im)
    a_shape = (1, seqlen, S5_GROUPS, S5_STATE)
    elems = (jnp.broadcast_to(abar_re, a_shape), jnp.broadcast_to(abar_im, a_shape), bu_re, bu_im)
    _, _, h_re, h_im = lax.associative_scan(_complex_scan_combine, elems, axis=1)
    y = (jnp.einsum('btgp,gcp->btgc', h_re, c_re.astype(f32))
         - jnp.einsum('btgp,gcp->btgc', h_im, c_im.astype(f32)))
    y = y.reshape(bsz, seqlen, dim) + d_skip.astype(f32) * x.astype(f32)
    y = jax.nn.gelu(y).astype(x.dtype)
    z = y @ w_glu + b_glu
    return z[..., :dim] * jax.nn.sigmoid(z[..., dim:])


def swiglu(x, w_gate, w_up, w_down):
    return (jax.nn.silu(x @ w_gate) * (x @ w_up)) @ w_down


def moe_swiglu(x, router, w_gate, w_up, w_down):
    bsz, seqlen, dim = x.shape
    xt = x.reshape(-1, dim)
    n_tok = xt.shape[0]
    logits = (xt @ router).astype(jnp.float32)
    top_val, top_idx = lax.top_k(logits, TOP_K)
    gates = jax.nn.softmax(top_val, axis=-1)
    n_assign = n_tok * TOP_K
    expert_flat = top_idx.reshape(-1)
    token_flat = jnp.arange(n_assign, dtype=jnp.int32) // TOP_K
    gate_flat = gates.reshape(-1)
    order = jnp.argsort(expert_flat)
    exp_sorted = expert_flat[order]
    counts = jnp.bincount(expert_flat, length=N_EXPERTS)
    padded = (counts + MOE_BLOCK - 1) // MOE_BLOCK * MOE_BLOCK
    start = jnp.cumsum(counts) - counts
    pad_end = jnp.cumsum(padded)
    pad_start = pad_end - padded
    dest = pad_start[exp_sorted] + (jnp.arange(n_assign, dtype=jnp.int32) - start[exp_sorted])
    n_blocks = (n_assign + MOE_BLOCK - 1) // MOE_BLOCK + N_EXPERTS
    cap = n_blocks * MOE_BLOCK
    buf_tok = jnp.zeros((cap,), jnp.int32).at[dest].set(token_flat[order])
    buf_gate = jnp.zeros((cap,), jnp.float32).at[dest].set(gate_flat[order])
    block_start = jnp.arange(n_blocks, dtype=jnp.int32) * MOE_BLOCK
    block_expert = jnp.minimum(jnp.searchsorted(pad_end, block_start, side='right'), N_EXPERTS - 1)

    def block_ffn(args):
        tok, gate, e = args
        xb = xt[tok]
        h = jax.nn.silu(xb @ w_gate[e]) * (xb @ w_up[e])
        return (h @ w_down[e]) * gate[:, None].astype(x.dtype)

    out_blocks = lax.map(block_ffn, (buf_tok.reshape(n_blocks, MOE_BLOCK),
                                     buf_gate.reshape(n_blocks, MOE_BLOCK), block_expert))
    y = jnp.zeros_like(xt).at[buf_tok].add(out_blocks.reshape(cap, dim))
    return y.reshape(bsz, seqlen, dim)


def setup_inputs(seed: int = 0) -> dict:
    key = jax.random.key(seed)
    ks = iter(jax.random.split(key, 48))
    nrm = lambda shape, scale: scale * jax.random.normal(next(ks), shape, jnp.float32)
    uni = lambda shape, lo, hi: jax.random.uniform(next(ks), shape, jnp.float32, lo, hi)
    D, H, N = D_MODEL, RWKV_HEADS, RWKV_HEAD_DIM
    G, P, Q = S5_GROUPS, S5_STATE, S5_GROUP
    NE, NO, NR = N_EVEN, N_ODD, N_EVEN - 1
    E, F, FE = N_EXPERTS, D_FF, D_EXPERT
    return {
        'x': nrm((BATCH, SEQ, D), 1.0),
        'norm_mix': 1.0 + nrm((DEPTH, D), 0.02),
        'norm_ffn': 1.0 + nrm((DEPTH, D), 0.02),
        'final_norm': 1.0 + nrm((D,), 0.02),
        'rwkv_mu': uni((NE, N_SHIFT_MIX, D), 0.0, 1.0),
        'rwkv_w_rkv': nrm((NE, 3, D, D), D ** -0.5),
        'rwkv_w0': uni((NE, D), -6.0, -1.0),
        'rwkv_w1': nrm((NE, D, LORA_DECAY), D ** -0.5),
        'rwkv_w2': nrm((NE, LORA_DECAY, D), 0.1 * LORA_DECAY ** -0.5),
        'rwkv_a0': nrm((NE, D), 0.5),
        'rwkv_a1': nrm((NE, D, LORA_AAA), D ** -0.5),
        'rwkv_a2': nrm((NE, LORA_AAA, D), 0.1 * LORA_AAA ** -0.5),
        'rwkv_g1': nrm((NE, D, LORA_GATE), D ** -0.5),
        'rwkv_g2': nrm((NE, LORA_GATE, D), LORA_GATE ** -0.5),
        'rwkv_k_k': 0.85 + nrm((NE, D), 0.05),
        'rwkv_k_a': 1.0 + nrm((NE, D), 0.05),
        'rwkv_r_k': nrm((NE, H, N), 0.1),
        'rwkv_lnx_w': 1.0 + nrm((NE, D), 0.05),
        'rwkv_lnx_b': nrm((NE, D), 0.01),
        'rwkv_w_o': nrm((NE, D, D), D ** -0.5),
        'rwkv_v0': 1.0 + nrm((NR, D), 0.1),
        'rwkv_v1': nrm((NR, D, LORA_MV), D ** -0.5),
        'rwkv_v2': nrm((NR, LORA_MV, D), 0.1 * LORA_MV ** -0.5),
        's5_lam_re': -0.5 + nrm((NO, G, P), 0.01),
        's5_lam_im': jnp.pi * jnp.arange(P, dtype=jnp.float32) + nrm((NO, G, P), 0.01),
        's5_log_step': uni((NO, G), math.log(S5_DT_MIN), math.log(S5_DT_MAX)),
        's5_b_re': nrm((NO, G, P, Q), (2.0 * Q) ** -0.5),
        's5_b_im': nrm((NO, G, P, Q), (2.0 * Q) ** -0.5),
        's5_c_re': nrm((NO, G, Q, P), (2.0 * P) ** -0.5),
        's5_c_im': nrm((NO, G, Q, P), (2.0 * P) ** -0.5),
        's5_d': nrm((NO, D), 1.0),
        's5_w_glu': nrm((NO, D, 2 * D), D ** -0.5),
        's5_b_glu': nrm((NO, 2 * D), 0.01),
        'ffn_gate': nrm((NE, D, F), D ** -0.5),
        'ffn_up': nrm((NE, D, F), D ** -0.5),
        'ffn_down': nrm((NE, F, D), F ** -0.5),
        'moe_router': nrm((NO, D, E), D ** -0.5),
        'moe_gate': nrm((NO, E, D, FE), D ** -0.5),
        'moe_up': nrm((NO, E, D, FE), D ** -0.5),
        'moe_down': nrm((NO, E, FE, D), FE ** -0.5),
    }


def reference(x, norm_mix, norm_ffn, final_norm,
              rwkv_mu, rwkv_w_rkv, rwkv_w0, rwkv_w1, rwkv_w2, rwkv_a0, rwkv_a1, rwkv_a2,
              rwkv_g1, rwkv_g2, rwkv_k_k, rwkv_k_a, rwkv_r_k, rwkv_lnx_w, rwkv_lnx_b, rwkv_w_o,
              rwkv_v0, rwkv_v1, rwkv_v2,
              s5_lam_re, s5_lam_im, s5_log_step, s5_b_re, s5_b_im, s5_c_re, s5_c_im, s5_d,
              s5_w_glu, s5_b_glu,
              ffn_gate, ffn_up, ffn_down,
              moe_router, moe_gate, moe_up, moe_down):
    h = x
    v_first = None
    for i in range(DEPTH):
        j = i // N_MIXERS
        hn = rmsnorm(h, norm_mix[i])
        if i % N_MIXERS == 0:
            v_mix = None if j == 0 else (rwkv_v0[j - 1], rwkv_v1[j - 1], rwkv_v2[j - 1])
            mix, v_first = rwkv7_time_mix(
                hn, rwkv_mu[j], rwkv_w_rkv[j], rwkv_w0[j], rwkv_w1[j], rwkv_w2[j],
                rwkv_a0[j], rwkv_a1[j], rwkv_a2[j], rwkv_g1[j], rwkv_g2[j],
                rwkv_k_k[j], rwkv_k_a[j], rwkv_r_k[j], rwkv_lnx_w[j], rwkv_lnx_b[j],
                rwkv_w_o[j], v_first, v_mix)
        else:
            mix = s5_mixer(hn, s5_lam_re[j], s5_lam_im[j], s5_log_step[j], s5_b_re[j],
                           s5_b_im[j], s5_c_re[j], s5_c_im[j], s5_d[j], s5_w_glu[j], s5_b_glu[j])
        h = h + mix.astype(h.dtype)
        hn = rmsnorm(h, norm_ffn[i])
        if i % 2 == 0:
            ffn = swiglu(hn, ffn_gate[j], ffn_up[j], ffn_down[j])
        else:
            ffn = moe_swiglu(hn, moe_router[j], moe_gate[j], moe_up[j], moe_down[j])
        h = h + ffn.astype(h.dtype)
    return rmsnorm(h, final_norm)
```

```python
import functools
import math

import jax
import jax.numpy as jnp
from jax import lax
from jax.experimental import pallas as pl
from jax.experimental.pallas import tpu as pltpu

F32 = jnp.float32
BF16 = jnp.bfloat16

NORM_EPS = 1e-6
LNX_EPS = 64e-5
HEAD_DIM = 64
S5_GROUP = 16
S5_STATE = 64
S5_EIG_MAX = -1e-4
N_EXPERTS = 8
LANES = 128
SUBLANES = 8
WKV_CHUNK = 64
VMEM_LIMIT = 56 * 1024 * 1024

PROJ_ROWS = 256
FFN_ROWS = 512
FFN_COLS = 1408
S5_ROWS = 512
ROUTER_ROWS = 512
MOE_ROWS = 512
MOE_COLS = 896
DMA_ROWS = 256
NEG_BIG = -1e30


def _params(*semantics):
    return pltpu.CompilerParams(dimension_semantics=semantics, vmem_limit_bytes=VMEM_LIMIT)


def _dot(a, b):
    return jnp.dot(a.astype(BF16), b.astype(BF16), preferred_element_type=F32)


def _dot_nt(a, b):
    return lax.dot_general(a.astype(BF16), b.astype(BF16), (((1,), (1,)), ((), ())),
                           preferred_element_type=F32)


def _dot_tn(a, b):
    return lax.dot_general(a.astype(BF16), b.astype(BF16), (((0,), (0,)), ((), ())),
                           preferred_element_type=F32)


def _split2(x):
    hi = x.astype(BF16)
    lo = (x - hi.astype(F32)).astype(BF16)
    return hi, lo


def _split3(x):
    hi = x.astype(BF16)
    r1 = x - hi.astype(F32)
    mid = r1.astype(BF16)
    lo = (r1 - mid.astype(F32)).astype(BF16)
    return hi, mid, lo


def _rms(x, g):
    return x * lax.rsqrt(jnp.mean(x * x, axis=-1, keepdims=True) + NORM_EPS) * g


def _sigmoid(x):
    return 1.0 / (1.0 + jnp.exp(-x))


def _full(shape):
    n = len(shape)
    return pl.BlockSpec(shape, lambda *_: (0,) * n)


def _rwkv_proj_kernel(*refs, has_vmix):
    (x_ref, xp_ref, gn_ref, mu_ref, wrkv_ref, w0_ref, w1_ref, w2_ref, a0_ref, a1_ref, a2_ref,
     g1_ref, g2_ref, kk_ref, ka_ref, hs_ref, hst_ref) = refs[:17]
    rest = refs[17:]
    if has_vmix:
        vf_ref, v0_ref, v1_ref, v2_ref = rest[:4]
        rest = rest[4:]
    r_o, lw_o, k_o, v_o, kk_o, b_o, g_o = rest

    i = pl.program_id(0)
    gn = gn_ref[...]
    hn = _rms(x_ref[...], gn)
    prev = _rms(xp_ref[...], gn) * jnp.where(i > 0, 1.0, 0.0)
    shifted = jnp.concatenate([prev, hn[:-SUBLANES]], axis=0)
    xx = shifted - hn
    xr, xk, xv, xw, xa, xg = [(hn + xx * mu_ref[s:s + 1, :]).astype(BF16) for s in range(6)]

    r = jnp.dot(xr, wrkv_ref[0], preferred_element_type=F32)
    k = jnp.dot(xk, wrkv_ref[1], preferred_element_type=F32)
    v = jnp.dot(xv, wrkv_ref[2], preferred_element_type=F32)

    z = w0_ref[...] + _dot(jnp.tanh(_dot(xw, w1_ref[...])), w2_ref[...])
    lw_o[...] = -math.exp(-0.5) * _sigmoid(z)

    if has_vmix:
        mixv = _sigmoid(v0_ref[...] + _dot(_dot(xv, v1_ref[...]), v2_ref[...]))
        v = v + (vf_ref[...].astype(F32) - v) * mixv
    a = _sigmoid(a0_ref[...] + _dot(_dot(xa, a1_ref[...]), a2_ref[...]))
    g = _dot(_sigmoid(_dot(xg, g1_ref[...])), g2_ref[...])

    kk = k * kk_ref[...]
    hi, lo = _split2(kk * kk)
    ss = (jnp.dot(hi, hs_ref[...], preferred_element_type=F32)
          + jnp.dot(lo, hs_ref[...], preferred_element_type=F32))
    inv = 1.0 / jnp.maximum(jnp.sqrt(ss), 1e-12)
    hi, lo = _split2(inv)
    inv_full = (jnp.dot(hi, hst_ref[...], preferred_element_type=F32)
                + jnp.dot(lo, hst_ref[...], preferred_element_type=F32))
    kk = kk * inv_full
    k = k * (1.0 + (a - 1.0) * ka_ref[...])

    r_o[...] = r.astype(BF16)
    k_o[...] = k.astype(BF16)
    v_o[...] = v.astype(BF16)
    kk_o[...] = kk.astype(BF16)
    b_o[...] = (kk * a).astype(BF16)
    g_o[...] = g.astype(BF16)


def _pad_cols(w, n):
    return jnp.pad(w, ((0, 0), (0, n - w.shape[1])))


def _pad_rows(w, n):
    return jnp.pad(w, ((0, n - w.shape[0]), (0, 0)))


def _rwkv_proj(h, gn, mu, w_rkv, w0, w1, w2, a0, a1, a2, g1, g2, k_k, k_a, v_first, v_mix):
    n, d = h.shape
    heads = d // HEAD_DIM
    row2 = lambda p: p.reshape(1, d).astype(F32)
    lora = lambda down, up: (_pad_cols(down, _rup(down.shape[1])).astype(BF16),
                             _pad_rows(up, _rup(up.shape[0])).astype(BF16))
    w1p, w2p = lora(w1, w2)
    a1p, a2p = lora(a1, a2)
    g1p, g2p = lora(g1, g2)
    head_of = jnp.arange(d, dtype=jnp.int32) // HEAD_DIM
    hs = (head_of[:, None] == jnp.arange(LANES, dtype=jnp.int32)[None, :]).astype(BF16)
    hst = hs.T
    has_vmix = v_mix is not None
    tm = PROJ_ROWS
    row_spec = pl.BlockSpec((tm, d), lambda i: (i, 0))
    prev_spec = pl.BlockSpec((SUBLANES, d), lambda i: (jnp.maximum(i * (tm // SUBLANES) - 1, 0), 0))
    args = [h, h, row2(gn), mu.astype(F32), w_rkv.astype(BF16), row2(w0), w1p, w2p, row2(a0), a1p, a2p,
            g1p, g2p, row2(k_k), row2(k_a), hs, hst]
    specs = [row_spec, prev_spec] + [_full(a.shape) for a in args[2:]]
    if has_vmix:
        v0, v1, v2 = v_mix
        v1p, v2p = lora(v1, v2)
        extra = [v_first, row2(v0), v1p, v2p]
        args += extra
        specs += [row_spec] + [_full(a.shape) for a in extra[1:]]
    bf = jax.ShapeDtypeStruct((n, d), BF16)
    out_shape = [bf, jax.ShapeDtypeStruct((n, d), F32), bf, bf, bf, bf, bf]
    return pl.pallas_call(
        functools.partial(_rwkv_proj_kernel, has_vmix=has_vmix),
        grid=(n // tm,),
        in_specs=specs,
        out_specs=[row_spec] * 7,
        out_shape=out_shape,
        compiler_params=_params("arbitrary"),
        name="rwkv_proj",
    )(*args)


def _rup(x, m=LANES):
    return (x + m - 1) // m * m


def _unit_lower_inverse(a, row, col):
    same = lambda nb: (row // nb) == (col // nb)
    eye = jnp.where(row == col, 1.0, 0.0)
    n1 = jnp.where(same(8), a, 0.0)
    n2 = _dot(n1, n1)
    n4 = _dot(n2, n2)
    t = eye + n1
    t = t + _dot(t, n2)
    t = t + _dot(t, n4)
    size = 8
    while size < a.shape[0]:
        off = jnp.where(jnp.logical_and(same(2 * size), jnp.logical_not(same(size))), a, 0.0)
        t = t + _dot(_dot(t, off), t)
        size *= 2
    return t


def _wkv_kernel(r_ref, lw_ref, k_ref, v_ref, kk_ref, b_ref, g_ref, rk_ref, lnw_ref, lnb_ref,
                o_ref, s_ref):
    @pl.when(pl.program_id(1) == 0)
    def _():
        s_ref[...] = jnp.zeros_like(s_ref)

    chunk, d = r_ref.shape
    heads = d // HEAD_DIM
    row = lax.broadcasted_iota(jnp.int32, (chunk, chunk), 0)
    col = lax.broadcasted_iota(jnp.int32, (chunk, chunk), 1)
    incl = row >= col
    strict = row > col

    lw = lw_ref[...]
    tri = jnp.where(incl, 1.0, 0.0).astype(BF16)
    lp = sum(jnp.dot(tri, part, preferred_element_type=F32) for part in _split3(lw))
    lp_end = lp[chunk - 1:chunk, :]
    r = r_ref[...].astype(F32)
    k = k_ref[...].astype(F32)
    v = v_ref[...].astype(F32)
    kk = kk_ref[...].astype(F32)
    b = b_ref[...].astype(F32)
    p_inv = jnp.exp(-lp)
    to_end = jnp.exp(lp_end - lp)
    a_t = (-kk * jnp.exp(lp - lw)).astype(BF16)
    b_t = (b * p_inv).astype(BF16)
    k_t = (k * p_inv).astype(BF16)
    r_t = (r * jnp.exp(lp)).astype(BF16)
    b_e = (b * to_end).astype(BF16)
    k_e = (k * to_end).astype(BF16)
    p_end = jnp.exp(lp_end)
    v_b = v_ref[...]
    rkk = r * k * rk_ref[...]
    lnw = lnw_ref[...]
    lnb = lnb_ref[...]
    g = g_ref[...].astype(F32)

    outs = []
    for h in range(heads):
        sl = slice(h * HEAD_DIM, (h + 1) * HEAD_DIM)
        s = s_ref[h]
        ah, bh, kh, rh, vh = a_t[:, sl], b_t[:, sl], k_t[:, sl], r_t[:, sl], v_b[:, sl]
        a_ab = jnp.where(strict, _dot_nt(ah, bh), 0.0)
        a_ak = jnp.where(strict, _dot_nt(ah, kh), 0.0)
        r_b = jnp.where(incl, _dot_nt(rh, bh), 0.0)
        r_k = jnp.where(incl, _dot_nt(rh, kh), 0.0)
        t_inv = _unit_lower_inverse(a_ab, row, col)
        u = _dot(t_inv, _dot_nt(ah, s) + _dot(a_ak, vh))
        y = _dot_nt(rh, s) + _dot(r_b, u) + _dot(r_k, vh)
        s_ref[h] = s * p_end[:, sl] + _dot_tn(u, b_e[:, sl]) + _dot_tn(vh, k_e[:, sl])

        mean = jnp.mean(y, axis=-1, keepdims=True)
        yc = y - mean
        var = jnp.mean(yc * yc, axis=-1, keepdims=True)
        yn = yc * lax.rsqrt(var + LNX_EPS) * lnw[:, sl] + lnb[:, sl]
        bonus = jnp.sum(rkk[:, sl], axis=-1, keepdims=True) * v[:, sl]
        outs.append((yn + bonus) * g[:, sl])
    o_ref[...] = jnp.concatenate(outs, axis=-1).astype(BF16)


def _wkv(r, lw, k, v, kk, b, g, r_k, lnx_w, lnx_b, batch):
    n, d = r.shape
    t = n // batch
    heads = d // HEAD_DIM
    view = lambda a: a.reshape(t, batch * d)
    blk = pl.BlockSpec((WKV_CHUNK, d), lambda bi, c: (c, bi))
    vec = pl.BlockSpec((1, d), lambda bi, c: (0, 0))
    row2 = lambda p: p.reshape(1, d).astype(F32)
    out = pl.pallas_call(
        _wkv_kernel,
        grid=(batch, t // WKV_CHUNK),
        in_specs=[blk] * 7 + [vec] * 3,
        out_specs=blk,
        out_shape=jax.ShapeDtypeStruct((t, batch * d), BF16),
        scratch_shapes=[pltpu.VMEM((heads, HEAD_DIM, HEAD_DIM), F32)],
        compiler_params=_params("arbitrary", "arbitrary"),
        name="wkv7",
    )(view(r), view(lw), view(k), view(v), view(kk), view(b), view(g), row2(r_k), row2(lnx_w), row2(lnx_b))
    return out.reshape(n, d)


def _wo_ffn_kernel(h_ref, yg_ref, wo_ref, gn_ref, wg_ref, wu_ref, wd_ref, o_ref, h1_s, hn_s, acc_s):
    f = pl.program_id(1)

    @pl.when(f == 0)
    def _():
        h1 = h_ref[...] + jnp.dot(yg_ref[...], wo_ref[...], preferred_element_type=F32)
        h1_s[...] = h1
        hn_s[...] = _rms(h1, gn_ref[...]).astype(BF16)
        acc_s[...] = jnp.zeros_like(acc_s)

    x = hn_s[...]
    gate = jnp.dot(x, wg_ref[...], preferred_element_type=F32)
    up = jnp.dot(x, wu_ref[...], preferred_element_type=F32)
    act = (gate * _sigmoid(gate) * up).astype(BF16)
    acc_s[...] += jnp.dot(act, wd_ref[...], preferred_element_type=F32)

    @pl.when(f == pl.num_programs(1) - 1)
    def _():
        o_ref[...] = h1_s[...] + acc_s[...]


def _wo_ffn(h, yg, w_o, gn, w_gate, w_up, w_down):
    n, d = h.shape
    ff = w_gate.shape[1]
    tm, tf = FFN_ROWS, FFN_COLS
    rows = pl.BlockSpec((tm, d), lambda i, f: (i, 0))
    return pl.pallas_call(
        _wo_ffn_kernel,
        grid=(n // tm, ff // tf),
        in_specs=[rows, rows, _full((d, d)), _full((1, d)),
                  pl.BlockSpec((d, tf), lambda i, f: (0, f)),
                  pl.BlockSpec((d, tf), lambda i, f: (0, f)),
                  pl.BlockSpec((tf, d), lambda i, f: (f, 0))],
        out_specs=rows,
        out_shape=jax.ShapeDtypeStruct((n, d), F32),
        scratch_shapes=[pltpu.VMEM((tm, d), F32), pltpu.VMEM((tm, d), BF16), pltpu.VMEM((tm, d), F32)],
        compiler_params=_params("arbitrary", "arbitrary"),
        name="wo_ffn",
    )(h, yg, w_o.astype(BF16), gn.reshape(1, d).astype(F32),
      w_gate.astype(BF16), w_up.astype(BF16), w_down.astype(BF16))


def _s5_disc_kernel(lr_ref, li_ref, dt_ref, bre_ref, bim_ref, are_o, aim_o, bbre_o, bbim_o):
    lr = jnp.minimum(lr_ref[...], S5_EIG_MAX)
    li = li_ref[...]
    dt = dt_ref[...]
    dr, di = lr * dt, li * dt
    e = jnp.exp(dr)
    abar_re = e * jnp.cos(di)
    abar_im = e * jnp.sin(di)
    sh = jnp.sin(0.5 * di)
    num_re = (e - 1.0) * jnp.cos(di) - 2.0 * sh * sh
    den = lr * lr + li * li
    coef_re = (num_re * lr + abar_im * li) / den
    coef_im = (abar_im * lr - num_re * li) / den
    are_o[...] = abar_re
    aim_o[...] = abar_im
    bre, bim = bre_ref[...], bim_ref[...]
    bbre_o[...] = coef_re * bre - coef_im * bim
    bbim_o[...] = coef_re * bim + coef_im * bre


def _s5_discretise(lam_re, lam_im, log_step, b_re, b_im):
    groups, states = lam_re.shape
    q = b_re.shape[-1]
    gp = groups * states
    col = lambda a: a.reshape(gp, 1).astype(F32)
    dt = jnp.exp(log_step.astype(F32))
    dt_col = jnp.broadcast_to(dt[:, None], (groups, states)).reshape(gp, 1)
    one = jax.ShapeDtypeStruct((gp, 1), F32)
    wide = jax.ShapeDtypeStruct((gp, q), F32)
    return pl.pallas_call(
        _s5_disc_kernel,
        out_shape=[one, one, wide, wide],
        name="s5_discretise",
    )(col(lam_re), col(lam_im), dt_col, b_re.reshape(gp, q).astype(F32), b_im.reshape(gp, q).astype(F32))


def _s5_glu_kernel(h_ref, gn_ref, wb_ref, are_ref, aim_ref, wcre_ref, wcim_ref, dsk_ref, wglu_ref, bglu_ref,
                   o_ref, st_s, bu_s, y_s):
    @pl.when(pl.program_id(0) == 0)
    def _():
        st_s[...] = jnp.zeros_like(st_s)

    tm, d = h_ref.shape
    slabs = d // LANES
    half = st_s.shape[-1] // 2
    steps = tm // SUBLANES
    x = h_ref[...]
    hn = _rms(x, gn_ref[...])
    dsk = dsk_ref[...]

    for s in range(slabs):
        sl = slice(s * LANES, (s + 1) * LANES)
        bu_s[...] = jnp.dot(hn[:, sl].astype(BF16), wb_ref[s], preferred_element_type=F32)
        a_re = are_ref[s]
        a_im = aim_ref[s]

        def step(t, carry):
            h_re, h_im = carry
            rows = pl.ds(pl.multiple_of(t * SUBLANES, SUBLANES), SUBLANES)
            n_re = a_re * h_re - a_im * h_im + bu_s[rows, :half]
            n_im = a_re * h_im + a_im * h_re + bu_s[rows, half:]
            bu_s[rows, :half] = n_re
            bu_s[rows, half:] = n_im
            return n_re, n_im

        h_re, h_im = lax.fori_loop(0, steps, step, (st_s[s, :, :half], st_s[s, :, half:]), unroll=8)
        st_s[s, :, :half] = h_re
        st_s[s, :, half:] = h_im
        y = (jnp.dot(bu_s[:, :half].astype(BF16), wcre_ref[s], preferred_element_type=F32)
             - jnp.dot(bu_s[:, half:].astype(BF16), wcim_ref[s], preferred_element_type=F32))
        y = y + dsk[:, sl] * hn[:, sl]
        y = 0.5 * y * (1.0 + jnp.tanh(math.sqrt(2.0 / math.pi) * (y + 0.044715 * (y * y * y))))
        y_s[:, sl] = y.astype(BF16)

    z = jnp.dot(y_s[...], wglu_ref[...], preferred_element_type=F32) + bglu_ref[...]
    o_ref[...] = x + z[:, :d] * _sigmoid(z[:, d:])


def _s5_glu(h, gn, lam_re, lam_im, log_step, b_re, b_im, c_re, c_im, d_skip, w_glu, b_glu):
    n, d = h.shape
    groups, states = lam_re.shape
    q = S5_GROUP
    slabs = d // LANES
    gl = LANES // q
    half = gl * states
    abar_re, abar_im, bbar_re, bbar_im = _s5_discretise(lam_re, lam_im, log_step, b_re, b_im)
    eye = jnp.eye(gl, dtype=F32)

    def in_proj(bb):
        bb = bb.reshape(slabs, gl, states, q)
        return jnp.einsum('sgpc,gh->sgchp', bb, eye).reshape(slabs, LANES, half)

    def out_proj(c):
        c = c.astype(F32).reshape(slabs, gl, q, states)
        return jnp.einsum('sgcp,gh->sgphc', c, eye).reshape(slabs, half, LANES).astype(BF16)

    wb = jnp.concatenate([in_proj(bbar_re), in_proj(bbar_im)], axis=-1).astype(BF16)
    lanes_of = lambda a: jnp.broadcast_to(a.reshape(slabs, 1, half), (slabs, SUBLANES, half))
    tm = S5_ROWS
    rows = pl.BlockSpec((tm, d), lambda i: (i, 0))
    args = [h, gn.reshape(1, d).astype(F32), wb, lanes_of(abar_re), lanes_of(abar_im),
            out_proj(c_re), out_proj(c_im), d_skip.reshape(1, d).astype(F32),
            w_glu.astype(BF16), b_glu.reshape(1, 2 * d).astype(F32)]
    return pl.pallas_call(
        _s5_glu_kernel,
        grid=(n // tm,),
        in_specs=[rows] + [_full(a.shape) for a in args[1:]],
        out_specs=rows,
        out_shape=jax.ShapeDtypeStruct((n, d), F32),
        scratch_shapes=[pltpu.VMEM((slabs, SUBLANES, 2 * half), F32),
                        pltpu.VMEM((tm, 2 * half), F32),
                        pltpu.VMEM((tm, d), BF16)],
        compiler_params=_params("arbitrary"),
        name="s5_glu",
    )(*args)


def _router_kernel(h_ref, gn_ref, rhi_ref, rlo_ref, xn_o, meta_o, gate_o, cnt_o, base_s):
    @pl.when(pl.program_id(0) == 0)
    def _():
        base_s[...] = jnp.zeros_like(base_s)

    tm = h_ref.shape[0]
    hn = _rms(h_ref[...], gn_ref[...])
    xn_o[...] = hn
    hi, lo = _split2(hn)
    logits = (jnp.dot(hi, rhi_ref[...], preferred_element_type=F32)
              + jnp.dot(hi, rlo_ref[...], preferred_element_type=F32)
              + jnp.dot(lo, rhi_ref[...], preferred_element_type=F32))
    lane = lax.broadcasted_iota(jnp.int32, (tm, LANES), 1).astype(F32)
    logits = jnp.where(lane < N_EXPERTS, logits, NEG_BIG)
    m1 = jnp.max(logits, axis=-1, keepdims=True)
    i1 = jnp.min(jnp.where(logits == m1, lane, float(LANES)), axis=-1, keepdims=True)
    rest = jnp.where(lane == i1, NEG_BIG, logits)
    m2 = jnp.max(rest, axis=-1, keepdims=True)
    i2 = jnp.min(jnp.where(rest == m2, lane, float(LANES)), axis=-1, keepdims=True)
    e = jnp.exp(m2 - m1)
    g1 = 1.0 / (1.0 + e)
    g2 = e / (1.0 + e)

    oh1 = jnp.where(lane == i1, 1.0, 0.0)
    oh2 = jnp.where(lane == i2, 1.0, 0.0)
    both = oh1 + oh2
    row = lax.broadcasted_iota(jnp.int32, (tm, tm), 0)
    col = lax.broadcasted_iota(jnp.int32, (tm, tm), 1)
    before = jnp.where(row > col, 1.0, 0.0).astype(BF16)
    seen = jnp.dot(before, both.astype(BF16), preferred_element_type=F32) + base_s[...]
    rank1 = jnp.sum(oh1 * seen, axis=-1, keepdims=True)
    rank2 = jnp.sum(oh2 * seen, axis=-1, keepdims=True)
    base = base_s[...] + jnp.sum(both, axis=0, keepdims=True)
    base_s[...] = base
    cnt_o[...] = jnp.broadcast_to(base, cnt_o.shape)

    meta = jnp.where(lane == 0.0, i1, jnp.where(lane == 1.0, i2, jnp.where(lane == 2.0, rank1, rank2)))
    meta_o[...] = meta.astype(jnp.int32)
    gate_o[...] = jnp.where(lane == 0.0, g1, g2)


def _router(h, gn, router):
    n, d = h.shape
    tm = ROUTER_ROWS
    rp = _pad_cols(router.astype(F32), LANES)
    rhi = rp.astype(BF16)
    rlo = (rp - rhi.astype(F32)).astype(BF16)
    rows = pl.BlockSpec((tm, d), lambda i: (i, 0))
    narrow = pl.BlockSpec((tm, LANES), lambda i: (i, 0))
    return pl.pallas_call(
        _router_kernel,
        grid=(n // tm,),
        in_specs=[rows, _full((1, d)), _full((d, LANES)), _full((d, LANES))],
        out_specs=[rows, narrow, narrow, _full((SUBLANES, LANES))],
        out_shape=[jax.ShapeDtypeStruct((n, d), F32),
                   jax.ShapeDtypeStruct((n, LANES), jnp.int32),
                   jax.ShapeDtypeStruct((n, LANES), F32),
                   jax.ShapeDtypeStruct((SUBLANES, LANES), F32)],
        scratch_shapes=[pltpu.VMEM((1, LANES), F32)],
        compiler_params=_params("arbitrary"),
        name="moe_router",
    )(h, gn.reshape(1, d).astype(F32), rhi, rlo)


def _row_copy(src_ref, src_row, dst_ref, dst_row, sem):
    return pltpu.make_async_copy(src_ref.at[pl.ds(src_row, 1)], dst_ref.at[pl.ds(dst_row, 1)], sem)


def _dispatch_kernel(dest_ref, x_ref, init_ref, xs_ref, sem):
    del init_ref
    tm = x_ref.shape[0]
    base = pl.program_id(0) * tm

    def issue(r, carry):
        for k in range(2):
            _row_copy(x_ref, r, xs_ref, dest_ref[2 * (base + r) + k], sem).start()
        return carry

    def drain(r, carry):
        for k in range(2):
            _row_copy(x_ref, 0, xs_ref, 0, sem).wait()
        return carry

    lax.fori_loop(0, tm, issue, 0)
    lax.fori_loop(0, tm, drain, 0)


def _dispatch(xn, dest, cap):
    n, d = xn.shape
    tm = DMA_ROWS
    return pl.pallas_call(
        _dispatch_kernel,
        grid_spec=pltpu.PrefetchScalarGridSpec(
            num_scalar_prefetch=1,
            grid=(n // tm,),
            in_specs=[pl.BlockSpec((tm, d), lambda i, dest: (i, 0)),
                      pl.BlockSpec(memory_space=pl.ANY)],
            out_specs=pl.BlockSpec(memory_space=pl.ANY),
            scratch_shapes=[pltpu.SemaphoreType.DMA(())],
        ),
        out_shape=jax.ShapeDtypeStruct((cap, d), F32),
        input_output_aliases={2: 0},
        compiler_params=_params("arbitrary"),
        name="moe_dispatch",
    )(dest, xn, jnp.zeros((cap, d), F32))


def _experts_kernel(te_ref, nu_ref, x_ref, wg_ref, wu_ref, wd_ref, o_ref, xb_s, acc_s):
    i = pl.program_id(0)
    f = pl.program_id(1)
    used = i < nu_ref[0]

    @pl.when(jnp.logical_and(used, f == 0))
    def _():
        xb_s[...] = x_ref[...].astype(BF16)
        acc_s[...] = jnp.zeros_like(acc_s)

    @pl.when(used)
    def _():
        x = xb_s[...]
        gate = jnp.dot(x, wg_ref[0], preferred_element_type=F32)
        up = jnp.dot(x, wu_ref[0], preferred_element_type=F32)
        act = (gate * _sigmoid(gate) * up).astype(BF16)
        acc_s[...] += jnp.dot(act, wd_ref[0], preferred_element_type=F32)

    @pl.when(f == pl.num_programs(1) - 1)
    def _():
        o_ref[...] = jnp.where(used, acc_s[...], 0.0)


def _experts(xs, tile_expert, n_used, w_gate, w_up, w_down):
    cap, d = xs.shape
    fe = w_gate.shape[-1]
    tm, tf = MOE_ROWS, MOE_COLS
    nf = fe // tf
    col = lambda i, f, te, nu: jnp.where(i < nu[0], f, nf - 1)
    rows = pl.BlockSpec((tm, d), lambda i, f, te, nu: (i, 0))
    return pl.pallas_call(
        _experts_kernel,
        grid_spec=pltpu.PrefetchScalarGridSpec(
            num_scalar_prefetch=2,
            grid=(cap // tm, nf),
            in_specs=[rows,
                      pl.BlockSpec((1, d, tf), lambda i, f, te, nu: (te[i], 0, col(i, f, te, nu))),
                      pl.BlockSpec((1, d, tf), lambda i, f, te, nu: (te[i], 0, col(i, f, te, nu))),
                      pl.BlockSpec((1, tf, d), lambda i, f, te, nu: (te[i], col(i, f, te, nu), 0))],
            out_specs=rows,
            scratch_shapes=[pltpu.VMEM((tm, d), BF16), pltpu.VMEM((tm, d), F32)],
        ),
        out_shape=jax.ShapeDtypeStruct((cap, d), F32),
        compiler_params=_params("arbitrary", "arbitrary"),
        name="moe_experts",
    )(tile_expert, n_used, xs, w_gate.astype(BF16), w_up.astype(BF16), w_down.astype(BF16))


def _combine_kernel(dest_ref, h_ref, gate_ref, fn_ref, eo_ref, o_ref, buf_s, sem, *, final_norm):
    tm = h_ref.shape[0]
    base = pl.program_id(0) * tm

    def issue(r, carry):
        for k in range(2):
            _row_copy(eo_ref, dest_ref[2 * (base + r) + k], buf_s.at[k], r, sem).start()
        return carry

    def drain(r, carry):
        for k in range(2):
            _row_copy(eo_ref, 0, buf_s.at[k], 0, sem).wait()
        return carry

    lax.fori_loop(0, tm, issue, 0)
    lax.fori_loop(0, tm, drain, 0)
    gates = gate_ref[...]
    out = h_ref[...] + gates[:, 0:1] * buf_s[0] + gates[:, 1:2] * buf_s[1]
    if final_norm:
        out = _rms(out, fn_ref[...])
    o_ref[...] = out


def _combine(h, gates, dest, eo, final_gain):
    n, d = h.shape
    tm = DMA_ROWS
    final_norm = final_gain is not None
    fn = (final_gain if final_norm else jnp.ones((d,), F32)).reshape(1, d).astype(F32)
    rows = pl.BlockSpec((tm, d), lambda i, dest: (i, 0))
    return pl.pallas_call(
        functools.partial(_combine_kernel, final_norm=final_norm),
        grid_spec=pltpu.PrefetchScalarGridSpec(
            num_scalar_prefetch=1,
            grid=(n // tm,),
            in_specs=[rows,
                      pl.BlockSpec((tm, LANES), lambda i, dest: (i, 0)),
                      pl.BlockSpec((1, d), lambda i, dest: (0, 0)),
                      pl.BlockSpec(memory_space=pl.ANY)],
            out_specs=rows,
            scratch_shapes=[pltpu.VMEM((2, tm, d), F32), pltpu.SemaphoreType.DMA(())],
        ),
        out_shape=jax.ShapeDtypeStruct((n, d), F32),
        compiler_params=_params("arbitrary"),
        name="moe_combine",
    )(dest, h, gates, fn, eo)


def _moe(h, gn, router, w_gate, w_up, w_down, final_gain):
    n, d = h.shape
    xn, meta, gates, counts = _router(h, gn, router)
    n_tiles = (2 * n) // MOE_ROWS + N_EXPERTS
    cap = n_tiles * MOE_ROWS
    cnt = counts[0, :N_EXPERTS].astype(jnp.int32)
    padded = (cnt + MOE_ROWS - 1) // MOE_ROWS * MOE_ROWS
    pad_end = jnp.cumsum(padded)
    pad_start = pad_end - padded
    dest = (pad_start[meta[:, 0:2]] + meta[:, 2:4]).reshape(-1).astype(jnp.int32)
    tile_start = jnp.arange(n_tiles, dtype=jnp.int32) * MOE_ROWS
    tile_expert = jnp.minimum(jnp.searchsorted(pad_end, tile_start, side='right'),
                              N_EXPERTS - 1).astype(jnp.int32)
    n_used = (pad_end[-1:] // MOE_ROWS).astype(jnp.int32)
    xs = _dispatch(xn, dest, cap)
    eo = _experts(xs, tile_expert, n_used, w_gate, w_up, w_down)
    return _combine(h, gates, dest, eo, final_gain)


def kernel(x, norm_mix, norm_ffn, final_norm, rwkv_mu, rwkv_w_rkv, rwkv_w0, rwkv_w1, rwkv_w2, rwkv_a0, rwkv_a1, rwkv_a2, rwkv_g1, rwkv_g2, rwkv_k_k, rwkv_k_a, rwkv_r_k, rwkv_lnx_w, rwkv_lnx_b, rwkv_w_o, rwkv_v0, rwkv_v1, rwkv_v2, s5_lam_re, s5_lam_im, s5_log_step, s5_b_re, s5_b_im, s5_c_re, s5_c_im, s5_d, s5_w_glu, s5_b_glu, ffn_gate, ffn_up, ffn_down, moe_router, moe_gate, moe_up, moe_down):
    batch, seq, d = x.shape
    depth = norm_mix.shape[0]
    assert batch == SUBLANES and d % LANES == 0 and seq % WKV_CHUNK == 0
    n = batch * seq
    h = jnp.transpose(x.astype(F32), (1, 0, 2)).reshape(n, d)
    v_first = None
    for i in range(depth):
        j = i // 2
        last = i == depth - 1
        if i % 2 == 0:
            v_mix = None if j == 0 else (rwkv_v0[j - 1], rwkv_v1[j - 1], rwkv_v2[j - 1])
            r, lw, k, v, kk, b, g = _rwkv_proj(
                h, norm_mix[i], rwkv_mu[j], rwkv_w_rkv[j], rwkv_w0[j], rwkv_w1[j], rwkv_w2[j],
                rwkv_a0[j], rwkv_a1[j], rwkv_a2[j], rwkv_g1[j], rwkv_g2[j], rwkv_k_k[j], rwkv_k_a[j],
                v_first, v_mix)
            if v_first is None:
                v_first = v
            yg = _wkv(r, lw, k, v, kk, b, g, rwkv_r_k[j], rwkv_lnx_w[j], rwkv_lnx_b[j], batch)
            h = _wo_ffn(h, yg, rwkv_w_o[j], norm_ffn[i], ffn_gate[j], ffn_up[j], ffn_down[j])
            if last:
                h = _final_norm(h, final_norm)
        else:
            h = _s5_glu(h, norm_mix[i], s5_lam_re[j], s5_lam_im[j], s5_log_step[j], s5_b_re[j], s5_b_im[j],
                        s5_c_re[j], s5_c_im[j], s5_d[j], s5_w_glu[j], s5_b_glu[j])
            h = _moe(h, norm_ffn[i], moe_router[j], moe_gate[j], moe_up[j], moe_down[j],
                     final_norm if last else None)
    return jnp.transpose(h.reshape(seq, batch, d), (1, 0, 2)).astype(x.dtype)


def _final_norm_kernel(h_ref, g_ref, o_ref):
    o_ref[...] = _rms(h_ref[...], g_ref[...])


def _final_norm(h, gain):
    n, d = h.shape
    rows = pl.BlockSpec((FFN_ROWS, d), lambda i: (i, 0))
    return pl.pallas_call(
        _final_norm_kernel, grid=(n // FFN_ROWS,), in_specs=[rows, _full((1, d))], out_specs=rows,
        out_shape=jax.ShapeDtypeStruct((n, d), F32), compiler_params=_params("arbitrary"),
        name="final_norm",
    )(h, gain.reshape(1, d).astype(F32))
```

```python
import functools
import math

import jax
import jax.numpy as jnp
from jax import lax
from jax.experimental import pallas as pl
from jax.experimental.pallas import tpu as pltpu

F32 = jnp.float32
BF16 = jnp.bfloat16

NORM_EPS = 1e-6
LNX_EPS = 64e-5
HEAD_DIM = 64
S5_GROUP = 16
S5_STATE = 64
S5_EIG_MAX = -1e-4
N_EXPERTS = 8
LANES = 128
SUBLANES = 8
WKV_CHUNK = 64
VMEM_LIMIT = 56 * 1024 * 1024

PROJ_ROWS = 256
FFN_ROWS = 512
FFN_COLS = 1408
S5_ROWS = 512
ROUTER_ROWS = 512
MOE_ROWS = 1024
MOE_COLS = 512
DMA_ROWS = 256
NEG_BIG = -1e30


def _params(*semantics):
    return pltpu.CompilerParams(dimension_semantics=semantics, vmem_limit_bytes=VMEM_LIMIT)


def _dot(a, b):
    return jnp.dot(a.astype(BF16), b.astype(BF16), preferred_element_type=F32)


def _dot_nt(a, b):
    return lax.dot_general(a.astype(BF16), b.astype(BF16), (((1,), (1,)), ((), ())),
                           preferred_element_type=F32)


def _dot_tn(a, b):
    return lax.dot_general(a.astype(BF16), b.astype(BF16), (((0,), (0,)), ((), ())),
                           preferred_element_type=F32)


def _split2(x):
    hi = x.astype(BF16)
    lo = (x - hi.astype(F32)).astype(BF16)
    return hi, lo


def _split3(x):
    hi = x.astype(BF16)
    r1 = x - hi.astype(F32)
    mid = r1.astype(BF16)
    lo = (r1 - mid.astype(F32)).astype(BF16)
    return hi, mid, lo


def _rms(x, g):
    return x * lax.rsqrt(jnp.mean(x * x, axis=-1, keepdims=True) + NORM_EPS) * g


def _sigmoid(x):
    return 1.0 / (1.0 + jnp.exp(-x))


def _full(shape):
    n = len(shape)
    return pl.BlockSpec(shape, lambda *_: (0,) * n)


def _rwkv_proj_kernel(*refs, has_vmix, tiles_per_seq):
    (x_ref, xp_ref, gn_ref, mu_ref, wrkv_ref, w0_ref, w1_ref, w2_ref, a0_ref, a1_ref, a2_ref,
     g1_ref, g2_ref, kk_ref, ka_ref, hs_ref, hst_ref) = refs[:17]
    rest = refs[17:]
    if has_vmix:
        vf_ref, v0_ref, v1_ref, v2_ref = rest[:4]
        rest = rest[4:]
    r_o, lw_o, k_o, v_o, kk_o, b_o, g_o = rest

    i = pl.program_id(0)
    gn = gn_ref[...]
    hn = _rms(x_ref[...], gn)
    starts_seq = (i % tiles_per_seq) == 0
    prev = _rms(xp_ref[SUBLANES - 1:SUBLANES, :], gn) * jnp.where(starts_seq, 0.0, 1.0)
    first_row = lax.broadcasted_iota(jnp.int32, hn.shape, 0) == 0
    shifted = jnp.where(first_row, prev, pltpu.roll(hn, 1, axis=0))
    xx = shifted - hn
    xr, xk, xv, xw, xa, xg = [(hn + xx * mu_ref[s:s + 1, :]).astype(BF16) for s in range(6)]

    r = jnp.dot(xr, wrkv_ref[0], preferred_element_type=F32)
    k = jnp.dot(xk, wrkv_ref[1], preferred_element_type=F32)
    v = jnp.dot(xv, wrkv_ref[2], preferred_element_type=F32)

    z = w0_ref[...] + _dot(jnp.tanh(_dot(xw, w1_ref[...])), w2_ref[...])
    lw_o[...] = -math.exp(-0.5) * _sigmoid(z)

    if has_vmix:
        mixv = _sigmoid(v0_ref[...] + _dot(_dot(xv, v1_ref[...]), v2_ref[...]))
        v = v + (vf_ref[...].astype(F32) - v) * mixv
    a = _sigmoid(a0_ref[...] + _dot(_dot(xa, a1_ref[...]), a2_ref[...]))
    g = _dot(_sigmoid(_dot(xg, g1_ref[...])), g2_ref[...])

    kk = k * kk_ref[...]
    hi, lo = _split2(kk * kk)
    ss = (jnp.dot(hi, hs_ref[...], preferred_element_type=F32)
          + jnp.dot(lo, hs_ref[...], preferred_element_type=F32))
    inv = 1.0 / jnp.maximum(jnp.sqrt(ss), 1e-12)
    hi, lo = _split2(inv)
    inv_full = (jnp.dot(hi, hst_ref[...], preferred_element_type=F32)
                + jnp.dot(lo, hst_ref[...], preferred_element_type=F32))
    kk = kk * inv_full
    k = k * (1.0 + (a - 1.0) * ka_ref[...])

    r_o[...] = r.astype(BF16)
    k_o[...] = k.astype(BF16)
    v_o[...] = v.astype(BF16)
    kk_o[...] = kk.astype(BF16)
    b_o[...] = (kk * a).astype(BF16)
    g_o[...] = g.astype(BF16)


def _pad_cols(w, n):
    return jnp.pad(w, ((0, 0), (0, n - w.shape[1])))


def _pad_rows(w, n):
    return jnp.pad(w, ((0, n - w.shape[0]), (0, 0)))


def _rwkv_proj(h, seq, gn, mu, w_rkv, w0, w1, w2, a0, a1, a2, g1, g2, k_k, k_a, v_first, v_mix):
    n, d = h.shape
    row2 = lambda p: p.reshape(1, d).astype(F32)
    lora = lambda down, up: (_pad_cols(down, _rup(down.shape[1])).astype(BF16),
                             _pad_rows(up, _rup(up.shape[0])).astype(BF16))
    w1p, w2p = lora(w1, w2)
    a1p, a2p = lora(a1, a2)
    g1p, g2p = lora(g1, g2)
    head_of = jnp.arange(d, dtype=jnp.int32) // HEAD_DIM
    hs = (head_of[:, None] == jnp.arange(LANES, dtype=jnp.int32)[None, :]).astype(BF16)
    hst = hs.T
    has_vmix = v_mix is not None
    tm = PROJ_ROWS
    assert seq % tm == 0
    row_spec = pl.BlockSpec((tm, d), lambda i: (i, 0))
    prev_spec = pl.BlockSpec((SUBLANES, d), lambda i: (jnp.maximum(i * (tm // SUBLANES) - 1, 0), 0))
    args = [h, h, row2(gn), mu.astype(F32), w_rkv.astype(BF16), row2(w0), w1p, w2p, row2(a0), a1p, a2p,
            g1p, g2p, row2(k_k), row2(k_a), hs, hst]
    specs = [row_spec, prev_spec] + [_full(a.shape) for a in args[2:]]
    if has_vmix:
        v0, v1, v2 = v_mix
        v1p, v2p = lora(v1, v2)
        extra = [v_first, row2(v0), v1p, v2p]
        args += extra
        specs += [row_spec] + [_full(a.shape) for a in extra[1:]]
    bf = jax.ShapeDtypeStruct((n, d), BF16)
    out_shape = [bf, jax.ShapeDtypeStruct((n, d), F32), bf, bf, bf, bf, bf]
    return pl.pallas_call(
        functools.partial(_rwkv_proj_kernel, has_vmix=has_vmix, tiles_per_seq=seq // tm),
        grid=(n // tm,),
        in_specs=specs,
        out_specs=[row_spec] * 7,
        out_shape=out_shape,
        compiler_params=_params("arbitrary"),
        name="rwkv_proj",
    )(*args)


def _rup(x, m=LANES):
    return (x + m - 1) // m * m


def _unit_lower_inverses(mats, row, col):
    same = lambda nb: (row // nb) == (col // nb)
    eye = jnp.where(row == col, 1.0, 0.0)
    n1 = [jnp.where(same(8), a, 0.0) for a in mats]
    n2 = [_dot(x, x) for x in n1]
    n4 = [_dot(x, x) for x in n2]
    t = [eye + x for x in n1]
    t = [ti + _dot(ti, x) for ti, x in zip(t, n2)]
    t = [ti + _dot(ti, x) for ti, x in zip(t, n4)]
    size = 8
    while size < WKV_CHUNK:
        grow = jnp.logical_and(same(2 * size), jnp.logical_not(same(size)))
        left = [_dot(ti, jnp.where(grow, a, 0.0)) for ti, a in zip(t, mats)]
        t = [ti + _dot(li, ti) for ti, li in zip(t, left)]
        size *= 2
    return t


def _wkv_kernel(r_ref, lw_ref, k_ref, v_ref, kk_ref, b_ref, g_ref, rk_ref, lnw_ref, lnb_ref,
                o_ref, s_ref):
    @pl.when(pl.program_id(1) == 0)
    def _():
        s_ref[...] = jnp.zeros_like(s_ref)

    chunk, d = r_ref.shape
    pairs = d // LANES
    pr = 2 * chunk
    row = lax.broadcasted_iota(jnp.int32, (pr, pr), 0)
    col = lax.broadcasted_iota(jnp.int32, (pr, pr), 1)
    same_head = (row // chunk) == (col // chunk)
    incl = jnp.logical_and(same_head, row >= col)
    strict = jnp.logical_and(same_head, row > col)
    crow = lax.broadcasted_iota(jnp.int32, (chunk, chunk), 0)
    ccol = lax.broadcasted_iota(jnp.int32, (chunk, chunk), 1)

    lw = lw_ref[...]
    tri = jnp.where(crow >= ccol, 1.0, 0.0).astype(BF16)
    lp = sum(jnp.dot(tri, part, preferred_element_type=F32) for part in _split3(lw))
    lp_end = lp[chunk - 1:chunk, :]
    r = r_ref[...].astype(F32)
    k = k_ref[...].astype(F32)
    v = v_ref[...].astype(F32)
    kk = kk_ref[...].astype(F32)
    b = b_ref[...].astype(F32)
    p_inv = jnp.exp(-lp)
    to_end = jnp.exp(lp_end - lp)
    p_end = jnp.exp(lp_end)
    a_t = -kk * jnp.exp(lp - lw)
    b_t = b * p_inv
    k_t = k * p_inv
    r_t = r * jnp.exp(lp)
    b_e = b * to_end
    k_e = k * to_end
    v_b = v_ref[...]

    lane = lambda p: slice(p * LANES, (p + 1) * LANES)

    def block_diag(x, p):
        xp = x[:, lane(p)]
        return jnp.where(same_head, jnp.concatenate([xp, xp], axis=0), 0.0).astype(BF16)

    def stack_heads(x):
        return jnp.concatenate([x[:, :HEAD_DIM], x[:, HEAD_DIM:]], axis=0)

    ps = range(pairs)
    ar = [jnp.concatenate([block_diag(a_t, p), block_diag(r_t, p)], axis=0) for p in ps]
    bk = [jnp.concatenate([block_diag(b_t, p), block_diag(k_t, p)], axis=0) for p in ps]
    bke = [jnp.concatenate([block_diag(b_e, p), block_diag(k_e, p)], axis=0) for p in ps]
    vs = [stack_heads(v_b[:, lane(p)]) for p in ps]
    s = [s_ref[p] for p in ps]
    gram = [_dot_nt(ar[p], bk[p]) for p in ps]
    ars = [_dot_nt(ar[p], s[p]) for p in ps]
    a_ab = [jnp.where(strict, gram[p][:pr, :pr], 0.0) for p in ps]
    a_ak = [jnp.where(strict, gram[p][:pr, pr:], 0.0) for p in ps]
    r_bk = [jnp.where(jnp.concatenate([incl, incl], axis=1), gram[p][pr:, :], 0.0).astype(BF16) for p in ps]
    w_in = [ars[p][:pr] + _dot(a_ak[p], vs[p]) for p in ps]
    t_inv = _unit_lower_inverses(a_ab, row, col)
    u = [_dot(t_inv[p], w_in[p]) for p in ps]
    uv = [jnp.concatenate([u[p].astype(BF16), vs[p]], axis=0) for p in ps]
    y = [ars[p][pr:] + jnp.dot(r_bk[p], uv[p], preferred_element_type=F32) for p in ps]
    for p in ps:
        s_ref[p] = s[p] * p_end[:, lane(p)] + _dot_tn(uv[p], bke[p])

    def head_norm(yh):
        mean = jnp.mean(yh, axis=-1, keepdims=True)
        yc = yh - mean
        var = jnp.mean(yc * yc, axis=-1, keepdims=True)
        return yc * lax.rsqrt(var + LNX_EPS)

    rkk = r * k * rk_ref[...]
    left = lax.broadcasted_iota(jnp.int32, (chunk, LANES), 1) < HEAD_DIM
    yn, dots = [], []
    for p in ps:
        yn.append(jnp.concatenate([head_norm(y[p][:chunk]), head_norm(y[p][chunk:])], axis=-1))
        rk_p = rkk[:, lane(p)]
        dots.append(jnp.where(left, jnp.sum(rk_p[:, :HEAD_DIM], axis=-1, keepdims=True),
                              jnp.sum(rk_p[:, HEAD_DIM:], axis=-1, keepdims=True)))
    yn = jnp.concatenate(yn, axis=-1)
    bonus = jnp.concatenate(dots, axis=-1) * v
    o_ref[...] = ((yn * lnw_ref[...] + lnb_ref[...] + bonus) * g_ref[...].astype(F32)).astype(BF16)


def _wkv(r, lw, k, v, kk, b, g, r_k, lnx_w, lnx_b, seq):
    n, d = r.shape
    chunks = seq // WKV_CHUNK
    blk = pl.BlockSpec((WKV_CHUNK, d), lambda bi, c: (bi * chunks + c, 0))
    vec = pl.BlockSpec((1, d), lambda bi, c: (0, 0))
    row2 = lambda p: p.reshape(1, d).astype(F32)
    return pl.pallas_call(
        _wkv_kernel,
        grid=(n // seq, chunks),
        in_specs=[blk] * 7 + [vec] * 3,
        out_specs=blk,
        out_shape=jax.ShapeDtypeStruct((n, d), BF16),
        scratch_shapes=[pltpu.VMEM((d // LANES, HEAD_DIM, LANES), F32)],
        compiler_params=_params("arbitrary", "arbitrary"),
        name="wkv7",
    )(r, lw, k, v, kk, b, g, row2(r_k), row2(lnx_w), row2(lnx_b))


def _wo_ffn_kernel(h_ref, yg_ref, wo_ref, gn_ref, wg_ref, wu_ref, wd_ref, o_ref, h1_s, hn_s, acc_s):
    f = pl.program_id(1)

    @pl.when(f == 0)
    def _():
        h1 = h_ref[...] + jnp.dot(yg_ref[...], wo_ref[...], preferred_element_type=F32)
        h1_s[...] = h1
        hn_s[...] = _rms(h1, gn_ref[...]).astype(BF16)
        acc_s[...] = jnp.zeros_like(acc_s)

    x = hn_s[...]
    gate = jnp.dot(x, wg_ref[...], preferred_element_type=F32)
    up = jnp.dot(x, wu_ref[...], preferred_element_type=F32)
    act = (gate * _sigmoid(gate) * up).astype(BF16)
    acc_s[...] += jnp.dot(act, wd_ref[...], preferred_element_type=F32)

    @pl.when(f == pl.num_programs(1) - 1)
    def _():
        o_ref[...] = h1_s[...] + acc_s[...]


def _wo_ffn(h, yg, w_o, gn, w_gate, w_up, w_down):
    n, d = h.shape
    ff = w_gate.shape[1]
    tm, tf = FFN_ROWS, FFN_COLS
    rows = pl.BlockSpec((tm, d), lambda i, f: (i, 0))
    return pl.pallas_call(
        _wo_ffn_kernel,
        grid=(n // tm, ff // tf),
        in_specs=[rows, rows, _full((d, d)), _full((1, d)),
                  pl.BlockSpec((d, tf), lambda i, f: (0, f)),
                  pl.BlockSpec((d, tf), lambda i, f: (0, f)),
                  pl.BlockSpec((tf, d), lambda i, f: (f, 0))],
        out_specs=rows,
        out_shape=jax.ShapeDtypeStruct((n, d), F32),
        scratch_shapes=[pltpu.VMEM((tm, d), F32), pltpu.VMEM((tm, d), BF16), pltpu.VMEM((tm, d), F32)],
        compiler_params=_params("arbitrary", "arbitrary"),
        name="wo_ffn",
    )(h, yg, w_o.astype(BF16), gn.reshape(1, d).astype(F32),
      w_gate.astype(BF16), w_up.astype(BF16), w_down.astype(BF16))


def _s5_disc_kernel(lr_ref, li_ref, dt_ref, bre_ref, bim_ref, are_o, aim_o, bbre_o, bbim_o):
    lr = jnp.minimum(lr_ref[...], S5_EIG_MAX)
    li = li_ref[...]
    dt = dt_ref[...]
    dr, di = lr * dt, li * dt
    e = jnp.exp(dr)
    abar_re = e * jnp.cos(di)
    abar_im = e * jnp.sin(di)
    sh = jnp.sin(0.5 * di)
    num_re = (e - 1.0) * jnp.cos(di) - 2.0 * sh * sh
    den = lr * lr + li * li
    coef_re = (num_re * lr + abar_im * li) / den
    coef_im = (abar_im * lr - num_re * li) / den
    are_o[...] = abar_re
    aim_o[...] = abar_im
    bre, bim = bre_ref[...], bim_ref[...]
    bbre_o[...] = coef_re * bre - coef_im * bim
    bbim_o[...] = coef_re * bim + coef_im * bre


def _s5_discretise(lam_re, lam_im, log_step, b_re, b_im):
    groups, states = lam_re.shape
    q = b_re.shape[-1]
    gp = groups * states
    col = lambda a: a.reshape(gp, 1).astype(F32)
    dt = jnp.exp(log_step.astype(F32))
    dt_col = jnp.broadcast_to(dt[:, None], (groups, states)).reshape(gp, 1)
    one = jax.ShapeDtypeStruct((gp, 1), F32)
    wide = jax.ShapeDtypeStruct((gp, q), F32)
    return pl.pallas_call(
        _s5_disc_kernel,
        out_shape=[one, one, wide, wide],
        name="s5_discretise",
    )(col(lam_re), col(lam_im), dt_col, b_re.reshape(gp, q).astype(F32), b_im.reshape(gp, q).astype(F32))


def _s5_glu_kernel(h_ref, gn_ref, wb_ref, are_ref, aim_ref, wcre_ref, wcim_ref, dsk_ref, wglu_ref, bglu_ref,
                   o_ref, st_s, xt_s, bu_s, y_s):
    @pl.when(pl.program_id(0) == 0)
    def _():
        st_s[...] = jnp.zeros_like(st_s)

    batch, steps, d = h_ref.shape
    slabs = d // LANES
    half = st_s.shape[-1] // 2
    lane = lambda s: slice(s * LANES, (s + 1) * LANES)
    gn = gn_ref[...]
    dsk = dsk_ref[...]

    for b in range(batch):
        hn_b = _rms(h_ref[b], gn)
        for s in range(slabs):
            xt_s[s, pl.ds(b, steps, stride=batch), :] = hn_b[:, lane(s)]

    for s in range(slabs):
        sl = lane(s)
        xs = xt_s[s]
        bu_s[...] = jnp.dot(xs.astype(BF16), wb_ref[s], preferred_element_type=F32)
        a_re = are_ref[s]
        a_im = aim_ref[s]

        def step(t, carry):
            h_re, h_im = carry
            rows = pl.ds(pl.multiple_of(t * batch, batch), batch)
            n_re = a_re * h_re - a_im * h_im + bu_s[rows, :half]
            n_im = a_re * h_im + a_im * h_re + bu_s[rows, half:]
            bu_s[rows, :half] = n_re
            bu_s[rows, half:] = n_im
            return n_re, n_im

        h_re, h_im = lax.fori_loop(0, steps, step, (st_s[s, :, :half], st_s[s, :, half:]), unroll=8)
        st_s[s, :, :half] = h_re
        st_s[s, :, half:] = h_im
        y = (jnp.dot(bu_s[:, :half].astype(BF16), wcre_ref[s], preferred_element_type=F32)
             - jnp.dot(bu_s[:, half:].astype(BF16), wcim_ref[s], preferred_element_type=F32))
        y = y + dsk[:, sl] * xs
        y = 0.5 * y * (1.0 + jnp.tanh(math.sqrt(2.0 / math.pi) * (y + 0.044715 * (y * y * y))))
        y_s[:, sl] = y.astype(BF16)

    z = jnp.dot(y_s[...], wglu_ref[...], preferred_element_type=F32) + bglu_ref[...]
    glu = z[:, :d] * _sigmoid(z[:, d:])
    for s in range(slabs):
        xt_s[s] = glu[:, lane(s)]
    for b in range(batch):
        for s in range(slabs):
            o_ref[b, :, lane(s)] = h_ref[b, :, lane(s)] + xt_s[s, pl.ds(b, steps, stride=batch), :]


def _s5_glu(h, seq, gn, lam_re, lam_im, log_step, b_re, b_im, c_re, c_im, d_skip, w_glu, b_glu):
    n, d = h.shape
    groups, states = lam_re.shape
    q = S5_GROUP
    slabs = d // LANES
    gl = LANES // q
    half = gl * states
    abar_re, abar_im, bbar_re, bbar_im = _s5_discretise(lam_re, lam_im, log_step, b_re, b_im)
    eye = jnp.eye(gl, dtype=F32)

    def in_proj(bb):
        bb = bb.reshape(slabs, gl, states, q)
        return jnp.einsum('sgpc,gh->sgchp', bb, eye).reshape(slabs, LANES, half)

    def out_proj(c):
        c = c.astype(F32).reshape(slabs, gl, q, states)
        return jnp.einsum('sgcp,gh->sgphc', c, eye).reshape(slabs, half, LANES).astype(BF16)

    wb = jnp.concatenate([in_proj(bbar_re), in_proj(bbar_im)], axis=-1).astype(BF16)
    lanes_of = lambda a: jnp.broadcast_to(a.reshape(slabs, 1, half), (slabs, SUBLANES, half))
    batch = n // seq
    assert batch == SUBLANES
    steps = S5_ROWS // batch
    tm = S5_ROWS
    rows = pl.BlockSpec((batch, steps, d), lambda i: (0, i, 0))
    args = [h.reshape(batch, seq, d), gn.reshape(1, d).astype(F32), wb, lanes_of(abar_re), lanes_of(abar_im),
            out_proj(c_re), out_proj(c_im), d_skip.reshape(1, d).astype(F32),
            w_glu.astype(BF16), b_glu.reshape(1, 2 * d).astype(F32)]
    out = pl.pallas_call(
        _s5_glu_kernel,
        grid=(seq // steps,),
        in_specs=[rows] + [_full(a.shape) for a in args[1:]],
        out_specs=rows,
        out_shape=jax.ShapeDtypeStruct((batch, seq, d), F32),
        scratch_shapes=[pltpu.VMEM((slabs, batch, 2 * half), F32),
                        pltpu.VMEM((slabs, tm, LANES), F32),
                        pltpu.VMEM((tm, 2 * half), F32),
                        pltpu.VMEM((tm, d), BF16)],
        compiler_params=_params("arbitrary"),
        name="s5_glu",
    )(*args)
    return out.reshape(n, d)


def _router_kernel(h_ref, gn_ref, rhi_ref, rlo_ref, xn_o, meta_o, gate_o, cnt_o, base_s):
    @pl.when(pl.program_id(0) == 0)
    def _():
        base_s[...] = jnp.zeros_like(base_s)

    tm = h_ref.shape[0]
    hn = _rms(h_ref[...], gn_ref[...])
    xn_o[...] = hn
    hi, lo = _split2(hn)
    logits = (jnp.dot(hi, rhi_ref[...], preferred_element_type=F32)
              + jnp.dot(hi, rlo_ref[...], preferred_element_type=F32)
              + jnp.dot(lo, rhi_ref[...], preferred_element_type=F32))
    lane = lax.broadcasted_iota(jnp.int32, (tm, LANES), 1).astype(F32)
    logits = jnp.where(lane < N_EXPERTS, logits, NEG_BIG)
    m1 = jnp.max(logits, axis=-1, keepdims=True)
    i1 = jnp.min(jnp.where(logits == m1, lane, float(LANES)), axis=-1, keepdims=True)
    rest = jnp.where(lane == i1, NEG_BIG, logits)
    m2 = jnp.max(rest, axis=-1, keepdims=True)
    i2 = jnp.min(jnp.where(rest == m2, lane, float(LANES)), axis=-1, keepdims=True)
    e = jnp.exp(m2 - m1)
    g1 = 1.0 / (1.0 + e)
    g2 = e / (1.0 + e)

    oh1 = jnp.where(lane == i1, 1.0, 0.0)
    oh2 = jnp.where(lane == i2, 1.0, 0.0)
    both = oh1 + oh2
    row = lax.broadcasted_iota(jnp.int32, (tm, tm), 0)
    col = lax.broadcasted_iota(jnp.int32, (tm, tm), 1)
    before = jnp.where(row > col, 1.0, 0.0).astype(BF16)
    seen = jnp.dot(before, both.astype(BF16), preferred_element_type=F32) + base_s[...]
    rank1 = jnp.sum(oh1 * seen, axis=-1, keepdims=True)
    rank2 = jnp.sum(oh2 * seen, axis=-1, keepdims=True)
    base = base_s[...] + jnp.sum(both, axis=0, keepdims=True)
    base_s[...] = base
    cnt_o[...] = jnp.broadcast_to(base, cnt_o.shape)

    meta = jnp.where(lane == 0.0, i1, jnp.where(lane == 1.0, i2, jnp.where(lane == 2.0, rank1, rank2)))
    meta_o[...] = meta.astype(jnp.int32)
    gate_o[...] = jnp.where(lane == 0.0, g1, g2)


def _router(h, gn, router):
    n, d = h.shape
    tm = ROUTER_ROWS
    rp = _pad_cols(router.astype(F32), LANES)
    rhi = rp.astype(BF16)
    rlo = (rp - rhi.astype(F32)).astype(BF16)
    rows = pl.BlockSpec((tm, d), lambda i: (i, 0))
    narrow = pl.BlockSpec((tm, LANES), lambda i: (i, 0))
    return pl.pallas_call(
        _router_kernel,
        grid=(n // tm,),
        in_specs=[rows, _full((1, d)), _full((d, LANES)), _full((d, LANES))],
        out_specs=[rows, narrow, narrow, _full((SUBLANES, LANES))],
        out_shape=[jax.ShapeDtypeStruct((n, d), F32),
                   jax.ShapeDtypeStruct((n, LANES), jnp.int32),
                   jax.ShapeDtypeStruct((n, LANES), F32),
                   jax.ShapeDtypeStruct((SUBLANES, LANES), F32)],
        scratch_shapes=[pltpu.VMEM((1, LANES), F32)],
        compiler_params=_params("arbitrary"),
        name="moe_router",
    )(h, gn.reshape(1, d).astype(F32), rhi, rlo)


def _row_copy(src_ref, src_row, dst_ref, dst_row, sem):
    return pltpu.make_async_copy(src_ref.at[pl.ds(src_row, 1)], dst_ref.at[pl.ds(dst_row, 1)], sem)


def _dispatch_kernel(dest_ref, x_ref, init_ref, xs_ref, sem):
    del init_ref
    tm = x_ref.shape[0]
    base = pl.program_id(0) * tm

    def issue(r, carry):
        for k in range(2):
            _row_copy(x_ref, r, xs_ref, dest_ref[2 * (base + r) + k], sem).start()
        return carry

    def drain(r, carry):
        for k in range(2):
            _row_copy(x_ref, 0, xs_ref, 0, sem).wait()
        return carry

    lax.fori_loop(0, tm, issue, 0)
    lax.fori_loop(0, tm, drain, 0)


def _dispatch(xn, dest, cap):
    n, d = xn.shape
    tm = DMA_ROWS
    return pl.pallas_call(
        _dispatch_kernel,
        grid_spec=pltpu.PrefetchScalarGridSpec(
            num_scalar_prefetch=1,
            grid=(n // tm,),
            in_specs=[pl.BlockSpec((tm, d), lambda i, dest: (i, 0)),
                      pl.BlockSpec(memory_space=pl.ANY)],
            out_specs=pl.BlockSpec(memory_space=pl.ANY),
            scratch_shapes=[pltpu.SemaphoreType.DMA(())],
        ),
        out_shape=jax.ShapeDtypeStruct((cap, d), F32),
        input_output_aliases={2: 0},
        compiler_params=_params("arbitrary"),
        name="moe_dispatch",
    )(dest, xn, jnp.zeros((cap, d), F32))


def _experts_kernel(te_ref, nu_ref, x_ref, wg_ref, wu_ref, wd_ref, o_ref, xb_s, acc_s):
    i = pl.program_id(0)
    f = pl.program_id(1)
    used = i < nu_ref[0]

    @pl.when(jnp.logical_and(used, f == 0))
    def _():
        xb_s[...] = x_ref[...].astype(BF16)
        acc_s[...] = jnp.zeros_like(acc_s)

    @pl.when(used)
    def _():
        x = xb_s[...]
        gate = jnp.dot(x, wg_ref[0].astype(BF16), preferred_element_type=F32)
        up = jnp.dot(x, wu_ref[0].astype(BF16), preferred_element_type=F32)
        act = (gate * _sigmoid(gate) * up).astype(BF16)
        acc_s[...] += jnp.dot(act, wd_ref[0].astype(BF16), preferred_element_type=F32)

    @pl.when(f == pl.num_programs(1) - 1)
    def _():
        o_ref[...] = jnp.where(used, acc_s[...], 0.0)


def _experts(xs, tile_expert, n_used, w_gate, w_up, w_down):
    cap, d = xs.shape
    fe = w_gate.shape[-1]
    tm, tf = MOE_ROWS, MOE_COLS
    nf = fe // tf
    col = lambda i, f, te, nu: jnp.where(i < nu[0], f, nf - 1)
    rows = pl.BlockSpec((tm, d), lambda i, f, te, nu: (i, 0))
    return pl.pallas_call(
        _experts_kernel,
        grid_spec=pltpu.PrefetchScalarGridSpec(
            num_scalar_prefetch=2,
            grid=(cap // tm, nf),
            in_specs=[rows,
                      pl.BlockSpec((1, d, tf), lambda i, f, te, nu: (te[i], 0, col(i, f, te, nu))),
                      pl.BlockSpec((1, d, tf), lambda i, f, te, nu: (te[i], 0, col(i, f, te, nu))),
                      pl.BlockSpec((1, tf, d), lambda i, f, te, nu: (te[i], col(i, f, te, nu), 0))],
            out_specs=rows,
            scratch_shapes=[pltpu.VMEM((tm, d), BF16), pltpu.VMEM((tm, d), F32)],
        ),
        out_shape=jax.ShapeDtypeStruct((cap, d), F32),
        compiler_params=_params("arbitrary", "arbitrary"),
        name="moe_experts",
    )(tile_expert, n_used, xs, w_gate, w_up, w_down)


def _combine_kernel(dest_ref, h_ref, gate_ref, fn_ref, eo_ref, o_ref, buf_s, sem, *, final_norm):
    tm = h_ref.shape[0]
    base = pl.program_id(0) * tm

    def issue(r, carry):
        for k in range(2):
            _row_copy(eo_ref, dest_ref[2 * (base + r) + k], buf_s.at[k], r, sem).start()
        return carry

    def drain(r, carry):
        for k in range(2):
            _row_copy(eo_ref, 0, buf_s.at[k], 0, sem).wait()
        return carry

    lax.fori_loop(0, tm, issue, 0)
    lax.fori_loop(0, tm, drain, 0)
    gates = gate_ref[...]
    out = h_ref[...] + gates[:, 0:1] * buf_s[0] + gates[:, 1:2] * buf_s[1]
    if final_norm:
        out = _rms(out, fn_ref[...])
    o_ref[...] = out


def _combine(h, gates, dest, eo, final_gain):
    n, d = h.shape
    tm = DMA_ROWS
    final_norm = final_gain is not None
    fn = (final_gain if final_norm else jnp.ones((d,), F32)).reshape(1, d).astype(F32)
    rows = pl.BlockSpec((tm, d), lambda i, dest: (i, 0))
    return pl.pallas_call(
        functools.partial(_combine_kernel, final_norm=final_norm),
        grid_spec=pltpu.PrefetchScalarGridSpec(
            num_scalar_prefetch=1,
            grid=(n // tm,),
            in_specs=[rows,
                      pl.BlockSpec((tm, LANES), lambda i, dest: (i, 0)),
                      pl.BlockSpec((1, d), lambda i, dest: (0, 0)),
                      pl.BlockSpec(memory_space=pl.ANY)],
            out_specs=rows,
            scratch_shapes=[pltpu.VMEM((2, tm, d), F32), pltpu.SemaphoreType.DMA(())],
        ),
        out_shape=jax.ShapeDtypeStruct((n, d), F32),
        compiler_params=_params("arbitrary"),
        name="moe_combine",
    )(dest, h, gates, fn, eo)


def _moe(h, gn, router, w_gate, w_up, w_down, final_gain):
    n, d = h.shape
    xn, meta, gates, counts = _router(h, gn, router)
    n_tiles = (2 * n) // MOE_ROWS + N_EXPERTS
    cap = n_tiles * MOE_ROWS
    cnt = counts[0, :N_EXPERTS].astype(jnp.int32)
    padded = (cnt + MOE_ROWS - 1) // MOE_ROWS * MOE_ROWS
    pad_end = jnp.cumsum(padded)
    pad_start = pad_end - padded
    dest = (pad_start[meta[:, 0:2]] + meta[:, 2:4]).reshape(-1).astype(jnp.int32)
    tile_start = jnp.arange(n_tiles, dtype=jnp.int32) * MOE_ROWS
    tile_expert = jnp.minimum(jnp.sum(tile_start[:, None] >= pad_end[None, :], axis=1),
                              N_EXPERTS - 1).astype(jnp.int32)
    n_used = (pad_end[-1:] // MOE_ROWS).astype(jnp.int32)
    xs = _dispatch(xn, dest, cap)
    eo = _experts(xs, tile_expert, n_used, w_gate, w_up, w_down)
    return _combine(h, gates, dest, eo, final_gain)


def kernel(x, norm_mix, norm_ffn, final_norm, rwkv_mu, rwkv_w_rkv, rwkv_w0, rwkv_w1, rwkv_w2, rwkv_a0, rwkv_a1, rwkv_a2, rwkv_g1, rwkv_g2, rwkv_k_k, rwkv_k_a, rwkv_r_k, rwkv_lnx_w, rwkv_lnx_b, rwkv_w_o, rwkv_v0, rwkv_v1, rwkv_v2, s5_lam_re, s5_lam_im, s5_log_step, s5_b_re, s5_b_im, s5_c_re, s5_c_im, s5_d, s5_w_glu, s5_b_glu, ffn_gate, ffn_up, ffn_down, moe_router, moe_gate, moe_up, moe_down):
    batch, seq, d = x.shape
    depth = norm_mix.shape[0]
    assert d % LANES == 0 and seq % WKV_CHUNK == 0
    n = batch * seq
    h = x.astype(F32).reshape(n, d)
    v_first = None
    for i in range(depth):
        j = i // 2
        last = i == depth - 1
        if i % 2 == 0:
            v_mix = None if j == 0 else (rwkv_v0[j - 1], rwkv_v1[j - 1], rwkv_v2[j - 1])
            r, lw, k, v, kk, b, g = _rwkv_proj(
                h, seq, norm_mix[i], rwkv_mu[j], rwkv_w_rkv[j], rwkv_w0[j], rwkv_w1[j], rwkv_w2[j],
                rwkv_a0[j], rwkv_a1[j], rwkv_a2[j], rwkv_g1[j], rwkv_g2[j], rwkv_k_k[j], rwkv_k_a[j],
                v_first, v_mix)
            if v_first is None:
                v_first = v
            yg = _wkv(r, lw, k, v, kk, b, g, rwkv_r_k[j], rwkv_lnx_w[j], rwkv_lnx_b[j], seq)
            h = _wo_ffn(h, yg, rwkv_w_o[j], norm_ffn[i], ffn_gate[j], ffn_up[j], ffn_down[j])
            if last:
                h = _final_norm(h, final_norm)
        else:
            h = _s5_glu(h, seq, norm_mix[i], s5_lam_re[j], s5_lam_im[j], s5_log_step[j], s5_b_re[j], s5_b_im[j],
                        s5_c_re[j], s5_c_im[j], s5_d[j], s5_w_glu[j], s5_b_glu[j])
            h = _moe(h, norm_ffn[i], moe_router[j], moe_gate[j], moe_up[j], moe_down[j],
                     final_norm if last else None)
    return h.reshape(batch, seq, d).astype(x.dtype)


def _final_norm_kernel(h_ref, g_ref, o_ref):
    o_ref[...] = _rms(h_ref[...], g_ref[...])


def _final_norm(h, gain):
    n, d = h.shape
    rows = pl.BlockSpec((FFN_ROWS, d), lambda i: (i, 0))
    return pl.pallas_call(
        _final_norm_kernel, grid=(n // FFN_ROWS,), in_specs=[rows, _full((1, d))], out_specs=rows,
        out_shape=jax.ShapeDtypeStruct((n, d), F32), compiler_params=_params("arbitrary"),
        name="final_norm",
    )(h, gain.reshape(1, d).astype(F32))
```

```python
import functools
import math

import jax
import jax.numpy as jnp
from jax import lax
from jax.experimental import pallas as pl
from jax.experimental.pallas import tpu as pltpu

F32 = jnp.float32
BF16 = jnp.bfloat16

NORM_EPS = 1e-6
LNX_EPS = 64e-5
HEAD_DIM = 64
S5_GROUP = 16
S5_STATE = 64
S5_EIG_MAX = -1e-4
N_EXPERTS = 8
LANES = 128
SUBLANES = 8
WKV_CHUNK = 64
WKV_BATCH_ROWS = 2
VMEM_LIMIT = 56 * 1024 * 1024

PROJ_ROWS = 256
FFN_ROWS = 512
FFN_COLS = 1408
S5_ROWS = 512
ROUTER_ROWS = 512
MOE_ROWS = 1024
MOE_COLS = 512
NEG_BIG = -1e30


def _params(*semantics):
    return pltpu.CompilerParams(dimension_semantics=semantics, vmem_limit_bytes=VMEM_LIMIT)


def _dot(a, b):
    return jnp.dot(a.astype(BF16), b.astype(BF16), preferred_element_type=F32)


def _dot_nt(a, b):
    return lax.dot_general(a.astype(BF16), b.astype(BF16), (((1,), (1,)), ((), ())),
                           preferred_element_type=F32)


def _dot_tn(a, b):
    return lax.dot_general(a.astype(BF16), b.astype(BF16), (((0,), (0,)), ((), ())),
                           preferred_element_type=F32)


def _split2(x):
    hi = x.astype(BF16)
    lo = (x - hi.astype(F32)).astype(BF16)
    return hi, lo


def _split3(x):
    hi = x.astype(BF16)
    r1 = x - hi.astype(F32)
    mid = r1.astype(BF16)
    lo = (r1 - mid.astype(F32)).astype(BF16)
    return hi, mid, lo


def _rms(x, g):
    return x * lax.rsqrt(jnp.mean(x * x, axis=-1, keepdims=True) + NORM_EPS) * g


def _sigmoid(x):
    return 1.0 / (1.0 + jnp.exp(-x))


def _full(shape):
    n = len(shape)
    return pl.BlockSpec(shape, lambda *_: (0,) * n)


def _rwkv_proj_kernel(*refs, has_vmix, tiles_per_seq):
    (x_ref, xp_ref, gn_ref, mu_ref, wrkv_ref, w0_ref, w1_ref, w2_ref, a0_ref, a1_ref, a2_ref,
     g1_ref, g2_ref, kk_ref, ka_ref, hs_ref, hst_ref) = refs[:17]
    rest = refs[17:]
    if has_vmix:
        vf_ref, v0_ref, v1_ref, v2_ref = rest[:4]
        rest = rest[4:]
    r_o, lw_o, k_o, v_o, kk_o, b_o, g_o = rest

    i = pl.program_id(0)
    gn = gn_ref[...]
    hn = _rms(x_ref[...], gn)
    starts_seq = (i % tiles_per_seq) == 0
    prev = _rms(xp_ref[SUBLANES - 1:SUBLANES, :], gn) * jnp.where(starts_seq, 0.0, 1.0)
    first_row = lax.broadcasted_iota(jnp.int32, hn.shape, 0) == 0
    shifted = jnp.where(first_row, prev, pltpu.roll(hn, 1, axis=0))
    xx = shifted - hn
    xr, xk, xv, xw, xa, xg = [(hn + xx * mu_ref[s:s + 1, :]).astype(BF16) for s in range(6)]

    r = jnp.dot(xr, wrkv_ref[0], preferred_element_type=F32)
    k = jnp.dot(xk, wrkv_ref[1], preferred_element_type=F32)
    v = jnp.dot(xv, wrkv_ref[2], preferred_element_type=F32)

    z = w0_ref[...] + _dot(jnp.tanh(_dot(xw, w1_ref[...])), w2_ref[...])
    lw_o[...] = -math.exp(-0.5) * _sigmoid(z)

    if has_vmix:
        mixv = _sigmoid(v0_ref[...] + _dot(_dot(xv, v1_ref[...]), v2_ref[...]))
        v = v + (vf_ref[...].astype(F32) - v) * mixv
    a = _sigmoid(a0_ref[...] + _dot(_dot(xa, a1_ref[...]), a2_ref[...]))
    g = _dot(_sigmoid(_dot(xg, g1_ref[...])), g2_ref[...])

    kk = k * kk_ref[...]
    hi, lo = _split2(kk * kk)
    ss = (jnp.dot(hi, hs_ref[...], preferred_element_type=F32)
          + jnp.dot(lo, hs_ref[...], preferred_element_type=F32))
    inv = 1.0 / jnp.maximum(jnp.sqrt(ss), 1e-12)
    hi, lo = _split2(inv)
    inv_full = (jnp.dot(hi, hst_ref[...], preferred_element_type=F32)
                + jnp.dot(lo, hst_ref[...], preferred_element_type=F32))
    kk = kk * inv_full
    k = k * (1.0 + (a - 1.0) * ka_ref[...])

    r_o[...] = r.astype(BF16)
    k_o[...] = k.astype(BF16)
    v_o[...] = v.astype(BF16)
    kk_o[...] = kk.astype(BF16)
    b_o[...] = (kk * a).astype(BF16)
    g_o[...] = g.astype(BF16)


def _pad_cols(w, n):
    return jnp.pad(w, ((0, 0), (0, n - w.shape[1])))


def _pad_rows(w, n):
    return jnp.pad(w, ((0, n - w.shape[0]), (0, 0)))


def _rwkv_proj(h, seq, gn, mu, w_rkv, w0, w1, w2, a0, a1, a2, g1, g2, k_k, k_a, v_first, v_mix):
    n, d = h.shape
    row2 = lambda p: p.reshape(1, d).astype(F32)
    lora = lambda down, up: (_pad_cols(down, _rup(down.shape[1])).astype(BF16),
                             _pad_rows(up, _rup(up.shape[0])).astype(BF16))
    w1p, w2p = lora(w1, w2)
    a1p, a2p = lora(a1, a2)
    g1p, g2p = lora(g1, g2)
    head_of = jnp.arange(d, dtype=jnp.int32) // HEAD_DIM
    hs = (head_of[:, None] == jnp.arange(LANES, dtype=jnp.int32)[None, :]).astype(BF16)
    hst = hs.T
    has_vmix = v_mix is not None
    tm = PROJ_ROWS
    assert seq % tm == 0
    row_spec = pl.BlockSpec((tm, d), lambda i: (i, 0))
    prev_spec = pl.BlockSpec((SUBLANES, d), lambda i: (jnp.maximum(i * (tm // SUBLANES) - 1, 0), 0))
    args = [h, h, row2(gn), mu.astype(F32), w_rkv.astype(BF16), row2(w0), w1p, w2p, row2(a0), a1p, a2p,
            g1p, g2p, row2(k_k), row2(k_a), hs, hst]
    specs = [row_spec, prev_spec] + [_full(a.shape) for a in args[2:]]
    if has_vmix:
        v0, v1, v2 = v_mix
        v1p, v2p = lora(v1, v2)
        extra = [v_first, row2(v0), v1p, v2p]
        args += extra
        specs += [row_spec] + [_full(a.shape) for a in extra[1:]]
    bf = jax.ShapeDtypeStruct((n, d), BF16)
    out_shape = [bf, jax.ShapeDtypeStruct((n, d), F32), bf, bf, bf, bf, bf]
    return pl.pallas_call(
        functools.partial(_rwkv_proj_kernel, has_vmix=has_vmix, tiles_per_seq=seq // tm),
        grid=(n // tm,),
        in_specs=specs,
        out_specs=[row_spec] * 7,
        out_shape=out_shape,
        compiler_params=_params("arbitrary"),
        name="rwkv_proj",
    )(*args)


def _rup(x, m=LANES):
    return (x + m - 1) // m * m


def _unit_lower_inverses(mats, row, col):
    same = lambda nb: (row // nb) == (col // nb)
    eye = jnp.where(row == col, 1.0, 0.0)
    n1 = [jnp.where(same(8), a, 0.0) for a in mats]
    n2 = [_dot(x, x) for x in n1]
    n4 = [_dot(x, x) for x in n2]
    t = [eye + x for x in n1]
    t = [ti + _dot(ti, x) for ti, x in zip(t, n2)]
    t = [ti + _dot(ti, x) for ti, x in zip(t, n4)]
    size = 8
    while size < WKV_CHUNK:
        grow = jnp.logical_and(same(2 * size), jnp.logical_not(same(size)))
        left = [_dot(ti, jnp.where(grow, a, 0.0)) for ti, a in zip(t, mats)]
        t = [ti + _dot(li, ti) for ti, li in zip(t, left)]
        size *= 2
    return t


def _wkv_kernel(r_ref, lw_ref, k_ref, v_ref, kk_ref, b_ref, g_ref, rk_ref, lnw_ref, lnb_ref,
                o_ref, s_ref):
    @pl.when(pl.program_id(1) == 0)
    def _():
        s_ref[...] = jnp.zeros_like(s_ref)

    nb, chunk, d = r_ref.shape
    pairs = d // LANES
    pr = 2 * chunk
    row = lax.broadcasted_iota(jnp.int32, (pr, pr), 0)
    col = lax.broadcasted_iota(jnp.int32, (pr, pr), 1)
    same_head = (row // chunk) == (col // chunk)
    incl = jnp.logical_and(same_head, row >= col)
    incl2 = jnp.concatenate([incl, incl], axis=1)
    strict = jnp.logical_and(same_head, row > col)
    crow = lax.broadcasted_iota(jnp.int32, (chunk, chunk), 0)
    ccol = lax.broadcasted_iota(jnp.int32, (chunk, chunk), 1)
    tri = jnp.where(crow >= ccol, 1.0, 0.0).astype(BF16)
    lane = lambda p: slice(p * LANES, (p + 1) * LANES)

    def block_diag(x, p):
        xp = x[:, lane(p)]
        return jnp.where(same_head, jnp.concatenate([xp, xp], axis=0), 0.0).astype(BF16)

    def stack_heads(x):
        return jnp.concatenate([x[:, :HEAD_DIM], x[:, HEAD_DIM:]], axis=0)

    ar, bk, bke, vs, p_end = [], [], [], [], []
    for bi in range(nb):
        lw = lw_ref[bi]
        lp = sum(jnp.dot(tri, part, preferred_element_type=F32) for part in _split3(lw))
        lp_end = lp[chunk - 1:chunk, :]
        r = r_ref[bi].astype(F32)
        k = k_ref[bi].astype(F32)
        kk = kk_ref[bi].astype(F32)
        b = b_ref[bi].astype(F32)
        p_inv = jnp.exp(-lp)
        to_end = jnp.exp(lp_end - lp)
        a_t = -kk * jnp.exp(lp - lw)
        b_t = b * p_inv
        k_t = k * p_inv
        r_t = r * jnp.exp(lp)
        b_e = b * to_end
        k_e = k * to_end
        v_b = v_ref[bi]
        for p in range(pairs):
            ar.append(jnp.concatenate([block_diag(a_t, p), block_diag(r_t, p)], axis=0))
            bk.append(jnp.concatenate([block_diag(b_t, p), block_diag(k_t, p)], axis=0))
            bke.append(jnp.concatenate([block_diag(b_e, p), block_diag(k_e, p)], axis=0))
            vs.append(stack_heads(v_b[:, lane(p)]))
            p_end.append(jnp.exp(lp_end[:, lane(p)]))

    qs = range(nb * pairs)
    s = [s_ref[q] for q in qs]
    gram = [_dot_nt(ar[q], bk[q]) for q in qs]
    ars = [_dot_nt(ar[q], s[q]) for q in qs]
    a_ab = [jnp.where(strict, gram[q][:pr, :pr], 0.0) for q in qs]
    a_ak = [jnp.where(strict, gram[q][:pr, pr:], 0.0) for q in qs]
    r_bk = [jnp.where(incl2, gram[q][pr:, :], 0.0).astype(BF16) for q in qs]
    w_in = [ars[q][:pr] + _dot(a_ak[q], vs[q]) for q in qs]
    t_inv = _unit_lower_inverses(a_ab, row, col)
    u = [_dot(t_inv[q], w_in[q]) for q in qs]
    uv = [jnp.concatenate([u[q].astype(BF16), vs[q]], axis=0) for q in qs]
    y = [ars[q][pr:] + jnp.dot(r_bk[q], uv[q], preferred_element_type=F32) for q in qs]
    for q in qs:
        s_ref[q] = s[q] * p_end[q] + _dot_tn(uv[q], bke[q])

    def head_norm(yh):
        mean = jnp.mean(yh, axis=-1, keepdims=True)
        yc = yh - mean
        var = jnp.mean(yc * yc, axis=-1, keepdims=True)
        return yc * lax.rsqrt(var + LNX_EPS)

    left = lax.broadcasted_iota(jnp.int32, (chunk, LANES), 1) < HEAD_DIM
    for bi in range(nb):
        rkk = r_ref[bi].astype(F32) * k_ref[bi].astype(F32) * rk_ref[...]
        yn, dots = [], []
        for p in range(pairs):
            yq = y[bi * pairs + p]
            yn.append(jnp.concatenate([head_norm(yq[:chunk]), head_norm(yq[chunk:])], axis=-1))
            rk_p = rkk[:, lane(p)]
            dots.append(jnp.where(left, jnp.sum(rk_p[:, :HEAD_DIM], axis=-1, keepdims=True),
                                  jnp.sum(rk_p[:, HEAD_DIM:], axis=-1, keepdims=True)))
        yn = jnp.concatenate(yn, axis=-1)
        bonus = jnp.concatenate(dots, axis=-1) * v_ref[bi].astype(F32)
        o_ref[bi] = ((yn * lnw_ref[...] + lnb_ref[...] + bonus) * g_ref[bi].astype(F32)).astype(BF16)


def _wkv(r, lw, k, v, kk, b, g, r_k, lnx_w, lnx_b, seq):
    n, d = r.shape
    batch = n // seq
    nb = WKV_BATCH_ROWS
    assert batch % nb == 0
    view = lambda a: a.reshape(batch, seq, d)
    blk = pl.BlockSpec((nb, WKV_CHUNK, d), lambda bi, c: (bi, c, 0))
    vec = pl.BlockSpec((1, d), lambda bi, c: (0, 0))
    row2 = lambda p: p.reshape(1, d).astype(F32)
    out = pl.pallas_call(
        _wkv_kernel,
        grid=(batch // nb, seq // WKV_CHUNK),
        in_specs=[blk] * 7 + [vec] * 3,
        out_specs=blk,
        out_shape=jax.ShapeDtypeStruct((batch, seq, d), BF16),
        scratch_shapes=[pltpu.VMEM((nb * (d // LANES), HEAD_DIM, LANES), F32)],
        compiler_params=_params("arbitrary", "arbitrary"),
        name="wkv7",
    )(view(r), view(lw), view(k), view(v), view(kk), view(b), view(g), row2(r_k), row2(lnx_w), row2(lnx_b))
    return out.reshape(n, d)


def _wo_ffn_kernel(h_ref, yg_ref, wo_ref, gn_ref, wg_ref, wu_ref, wd_ref, o_ref, h1_s, hn_s, acc_s):
    f = pl.program_id(1)

    @pl.when(f == 0)
    def _():
        h1 = h_ref[...] + jnp.dot(yg_ref[...], wo_ref[...], preferred_element_type=F32)
        h1_s[...] = h1
        hn_s[...] = _rms(h1, gn_ref[...]).astype(BF16)
        acc_s[...] = jnp.zeros_like(acc_s)

    x = hn_s[...]
    gate = jnp.dot(x, wg_ref[...], preferred_element_type=F32)
    up = jnp.dot(x, wu_ref[...], preferred_element_type=F32)
    act = (gate * _sigmoid(gate) * up).astype(BF16)
    acc_s[...] += jnp.dot(act, wd_ref[...], preferred_element_type=F32)

    @pl.when(f == pl.num_programs(1) - 1)
    def _():
        o_ref[...] = h1_s[...] + acc_s[...]


def _wo_ffn(h, yg, w_o, gn, w_gate, w_up, w_down):
    n, d = h.shape
    ff = w_gate.shape[1]
    tm, tf = FFN_ROWS, FFN_COLS
    rows = pl.BlockSpec((tm, d), lambda i, f: (i, 0))
    return pl.pallas_call(
        _wo_ffn_kernel,
        grid=(n // tm, ff // tf),
        in_specs=[rows, rows, _full((d, d)), _full((1, d)),
                  pl.BlockSpec((d, tf), lambda i, f: (0, f)),
                  pl.BlockSpec((d, tf), lambda i, f: (0, f)),
                  pl.BlockSpec((tf, d), lambda i, f: (f, 0))],
        out_specs=rows,
        out_shape=jax.ShapeDtypeStruct((n, d), F32),
        scratch_shapes=[pltpu.VMEM((tm, d), F32), pltpu.VMEM((tm, d), BF16), pltpu.VMEM((tm, d), F32)],
        compiler_params=_params("arbitrary", "arbitrary"),
        name="wo_ffn",
    )(h, yg, w_o.astype(BF16), gn.reshape(1, d).astype(F32),
      w_gate.astype(BF16), w_up.astype(BF16), w_down.astype(BF16))


def _s5_disc_kernel(lr_ref, li_ref, dt_ref, bre_ref, bim_ref, are_o, aim_o, bbre_o, bbim_o):
    lr = jnp.minimum(lr_ref[...], S5_EIG_MAX)
    li = li_ref[...]
    dt = dt_ref[...]
    dr, di = lr * dt, li * dt
    e = jnp.exp(dr)
    abar_re = e * jnp.cos(di)
    abar_im = e * jnp.sin(di)
    sh = jnp.sin(0.5 * di)
    num_re = (e - 1.0) * jnp.cos(di) - 2.0 * sh * sh
    den = lr * lr + li * li
    coef_re = (num_re * lr + abar_im * li) / den
    coef_im = (abar_im * lr - num_re * li) / den
    are_o[...] = abar_re
    aim_o[...] = abar_im
    bre, bim = bre_ref[...], bim_ref[...]
    bbre_o[...] = coef_re * bre - coef_im * bim
    bbim_o[...] = coef_re * bim + coef_im * bre


def _s5_discretise(lam_re, lam_im, log_step, b_re, b_im):
    groups, states = lam_re.shape
    q = b_re.shape[-1]
    gp = groups * states
    col = lambda a: a.reshape(gp, 1).astype(F32)
    dt = jnp.exp(log_step.astype(F32))
    dt_col = jnp.broadcast_to(dt[:, None], (groups, states)).reshape(gp, 1)
    one = jax.ShapeDtypeStruct((gp, 1), F32)
    wide = jax.ShapeDtypeStruct((gp, q), F32)
    return pl.pallas_call(
        _s5_disc_kernel,
        out_shape=[one, one, wide, wide],
        name="s5_discretise",
    )(col(lam_re), col(lam_im), dt_col, b_re.reshape(gp, q).astype(F32), b_im.reshape(gp, q).astype(F32))


def _s5_glu_kernel(h_ref, gn_ref, wb_ref, are_ref, aim_ref, wcre_ref, wcim_ref, dsk_ref, wglu_ref, bglu_ref,
                   o_ref, st_s, xt_s, bu_s, y_s):
    @pl.when(pl.program_id(0) == 0)
    def _():
        st_s[...] = jnp.zeros_like(st_s)

    batch, steps, d = h_ref.shape
    slabs = d // LANES
    half = st_s.shape[-1] // 2
    lane = lambda s: slice(s * LANES, (s + 1) * LANES)
    gn = gn_ref[...]
    dsk = dsk_ref[...]

    for b in range(batch):
        hn_b = _rms(h_ref[b], gn)
        for s in range(slabs):
            xt_s[s, pl.ds(b, steps, stride=batch), :] = hn_b[:, lane(s)]

    for s in range(slabs):
        sl = lane(s)
        xs = xt_s[s]
        bu_s[...] = jnp.dot(xs.astype(BF16), wb_ref[s], preferred_element_type=F32)
        a_re = are_ref[s]
        a_im = aim_ref[s]

        def step(t, carry):
            h_re, h_im = carry
            rows = pl.ds(pl.multiple_of(t * batch, batch), batch)
            n_re = a_re * h_re - a_im * h_im + bu_s[rows, :half]
            n_im = a_re * h_im + a_im * h_re + bu_s[rows, half:]
            bu_s[rows, :half] = n_re
            bu_s[rows, half:] = n_im
            return n_re, n_im

        h_re, h_im = lax.fori_loop(0, steps, step, (st_s[s, :, :half], st_s[s, :, half:]), unroll=8)
        st_s[s, :, :half] = h_re
        st_s[s, :, half:] = h_im
        y = (jnp.dot(bu_s[:, :half].astype(BF16), wcre_ref[s], preferred_element_type=F32)
             - jnp.dot(bu_s[:, half:].astype(BF16), wcim_ref[s], preferred_element_type=F32))
        y = y + dsk[:, sl] * xs
        y = 0.5 * y * (1.0 + jnp.tanh(math.sqrt(2.0 / math.pi) * (y + 0.044715 * (y * y * y))))
        y_s[:, sl] = y.astype(BF16)

    z = jnp.dot(y_s[...], wglu_ref[...], preferred_element_type=F32) + bglu_ref[...]
    glu = z[:, :d] * _sigmoid(z[:, d:])
    for s in range(slabs):
        xt_s[s] = glu[:, lane(s)]
    for b in range(batch):
        for s in range(slabs):
            o_ref[b, :, lane(s)] = h_ref[b, :, lane(s)] + xt_s[s, pl.ds(b, steps, stride=batch), :]


def _s5_glu(h, seq, gn, lam_re, lam_im, log_step, b_re, b_im, c_re, c_im, d_skip, w_glu, b_glu):
    n, d = h.shape
    groups, states = lam_re.shape
    q = S5_GROUP
    slabs = d // LANES
    gl = LANES // q
    half = gl * states
    abar_re, abar_im, bbar_re, bbar_im = _s5_discretise(lam_re, lam_im, log_step, b_re, b_im)
    eye = jnp.eye(gl, dtype=F32)

    def in_proj(bb):
        bb = bb.reshape(slabs, gl, states, q)
        return jnp.einsum('sgpc,gh->sgchp', bb, eye).reshape(slabs, LANES, half)

    def out_proj(c):
        c = c.astype(F32).reshape(slabs, gl, q, states)
        return jnp.einsum('sgcp,gh->sgphc', c, eye).reshape(slabs, half, LANES).astype(BF16)

    wb = jnp.concatenate([in_proj(bbar_re), in_proj(bbar_im)], axis=-1).astype(BF16)
    lanes_of = lambda a: jnp.broadcast_to(a.reshape(slabs, 1, half), (slabs, SUBLANES, half))
    batch = n // seq
    assert batch == SUBLANES
    steps = S5_ROWS // batch
    tm = S5_ROWS
    rows = pl.BlockSpec((batch, steps, d), lambda i: (0, i, 0))
    args = [h.reshape(batch, seq, d), gn.reshape(1, d).astype(F32), wb, lanes_of(abar_re), lanes_of(abar_im),
            out_proj(c_re), out_proj(c_im), d_skip.reshape(1, d).astype(F32),
            w_glu.astype(BF16), b_glu.reshape(1, 2 * d).astype(F32)]
    out = pl.pallas_call(
        _s5_glu_kernel,
        grid=(seq // steps,),
        in_specs=[rows] + [_full(a.shape) for a in args[1:]],
        out_specs=rows,
        out_shape=jax.ShapeDtypeStruct((batch, seq, d), F32),
        scratch_shapes=[pltpu.VMEM((slabs, batch, 2 * half), F32),
                        pltpu.VMEM((slabs, tm, LANES), F32),
                        pltpu.VMEM((tm, 2 * half), F32),
                        pltpu.VMEM((tm, d), BF16)],
        compiler_params=_params("arbitrary"),
        name="s5_glu",
    )(*args)
    return out.reshape(n, d)


def _router_kernel(h_ref, gn_ref, rhi_ref, rlo_ref, xn_o, meta_o, gate_o, cnt_o, base_s):
    @pl.when(pl.program_id(0) == 0)
    def _():
        base_s[...] = jnp.zeros_like(base_s)

    tm = h_ref.shape[0]
    hn = _rms(h_ref[...], gn_ref[...])
    xn_o[...] = hn
    hi, lo = _split2(hn)
    logits = (jnp.dot(hi, rhi_ref[...], preferred_element_type=F32)
              + jnp.dot(hi, rlo_ref[...], preferred_element_type=F32)
              + jnp.dot(lo, rhi_ref[...], preferred_element_type=F32))
    lane = lax.broadcasted_iota(jnp.int32, (tm, LANES), 1).astype(F32)
    logits = jnp.where(lane < N_EXPERTS, logits, NEG_BIG)
    m1 = jnp.max(logits, axis=-1, keepdims=True)
    i1 = jnp.min(jnp.where(logits == m1, lane, float(LANES)), axis=-1, keepdims=True)
    rest = jnp.where(lane == i1, NEG_BIG, logits)
    m2 = jnp.max(rest, axis=-1, keepdims=True)
    i2 = jnp.min(jnp.where(rest == m2, lane, float(LANES)), axis=-1, keepdims=True)
    e = jnp.exp(m2 - m1)
    g1 = 1.0 / (1.0 + e)
    g2 = e / (1.0 + e)

    oh1 = jnp.where(lane == i1, 1.0, 0.0)
    oh2 = jnp.where(lane == i2, 1.0, 0.0)
    both = oh1 + oh2
    row = lax.broadcasted_iota(jnp.int32, (tm, tm), 0)
    col = lax.broadcasted_iota(jnp.int32, (tm, tm), 1)
    before = jnp.where(row > col, 1.0, 0.0).astype(BF16)
    seen = jnp.dot(before, both.astype(BF16), preferred_element_type=F32) + base_s[...]
    rank1 = jnp.sum(oh1 * seen, axis=-1, keepdims=True)
    rank2 = jnp.sum(oh2 * seen, axis=-1, keepdims=True)
    base = base_s[...] + jnp.sum(both, axis=0, keepdims=True)
    base_s[...] = base
    cnt_o[...] = jnp.broadcast_to(base, cnt_o.shape)

    meta = jnp.where(lane == 0.0, i1, jnp.where(lane == 1.0, i2, jnp.where(lane == 2.0, rank1, rank2)))
    meta_o[...] = meta.astype(jnp.int32)
    gate_o[...] = jnp.where(lane == 0.0, g1, g2)


def _router(h, gn, router):
    n, d = h.shape
    tm = ROUTER_ROWS
    rp = _pad_cols(router.astype(F32), LANES)
    rhi = rp.astype(BF16)
    rlo = (rp - rhi.astype(F32)).astype(BF16)
    rows = pl.BlockSpec((tm, d), lambda i: (i, 0))
    narrow = pl.BlockSpec((tm, LANES), lambda i: (i, 0))
    return pl.pallas_call(
        _router_kernel,
        grid=(n // tm,),
        in_specs=[rows, _full((1, d)), _full((d, LANES)), _full((d, LANES))],
        out_specs=[rows, narrow, narrow, _full((SUBLANES, LANES))],
        out_shape=[jax.ShapeDtypeStruct((n, d), F32),
                   jax.ShapeDtypeStruct((n, LANES), jnp.int32),
                   jax.ShapeDtypeStruct((n, LANES), F32),
                   jax.ShapeDtypeStruct((SUBLANES, LANES), F32)],
        scratch_shapes=[pltpu.VMEM((1, LANES), F32)],
        compiler_params=_params("arbitrary"),
        name="moe_router",
    )(h, gn.reshape(1, d).astype(F32), rhi, rlo)


def _row_copy(src_ref, src_row, dst_ref, dst_row, sem):
    return pltpu.make_async_copy(src_ref.at[pl.ds(src_row, 1)], dst_ref.at[pl.ds(dst_row, 1)], sem)


def _experts_kernel(te_ref, nu_ref, src_ref, dst_ref, xn_ref, wg_ref, wu_ref, wd_ref, out_ref,
                    xg_s, xb_s, acc_s, ob_s, gsem, osem):
    del te_ref
    i = pl.program_id(0)
    f = pl.program_id(1)
    n_used = nu_ref[0]
    used = i < n_used
    tm = acc_s.shape[0]

    def start_gather(tile, slot):
        def body(r, carry):
            _row_copy(xn_ref, src_ref[tile * tm + r], xg_s.at[slot], r, gsem.at[slot]).start()
            return carry
        lax.fori_loop(0, tm, body, 0, unroll=8)

    def wait_scatter():
        pltpu.make_async_copy(ob_s, out_ref.at[pl.ds(0, tm)], osem).wait()

    @pl.when(jnp.logical_and(used, f == 0))
    def _():
        slot = i % 2

        @pl.when(i == 0)
        def _():
            start_gather(0, 0)
            ob_s[...] = jnp.zeros_like(ob_s)
            plane = out_ref.shape[0] // 2
            for k in range(2):
                pltpu.make_async_copy(ob_s, out_ref.at[pl.ds((k + 1) * plane - tm, tm)], osem).start()
            for k in range(2):
                wait_scatter()

        pltpu.make_async_copy(xn_ref.at[pl.ds(0, tm)], xg_s.at[slot], gsem.at[slot]).wait()

        @pl.when(i + 1 < n_used)
        def _():
            start_gather(i + 1, 1 - slot)

        xb_s[...] = xg_s[slot].astype(BF16)
        acc_s[...] = jnp.zeros_like(acc_s)

    @pl.when(used)
    def _():
        x = xb_s[...]
        gate = jnp.dot(x, wg_ref[0, 0].astype(BF16), preferred_element_type=F32)
        up = jnp.dot(x, wu_ref[0, 0].astype(BF16), preferred_element_type=F32)
        act = (gate * _sigmoid(gate) * up).astype(BF16)
        acc_s[...] += jnp.dot(act, wd_ref[0, 0].astype(BF16), preferred_element_type=F32)

    @pl.when(jnp.logical_and(used, f == pl.num_programs(1) - 1))
    def _():
        @pl.when(i > 0)
        def _():
            wait_scatter()

        ob_s[...] = acc_s[...]

        def body(r, carry):
            _row_copy(ob_s, r, out_ref, dst_ref[i * tm + r], osem).start()
            return carry
        lax.fori_loop(0, tm, body, 0, unroll=8)

        @pl.when(i == n_used - 1)
        def _():
            wait_scatter()


def _experts(xn, tile_expert, n_used, src, dst, layer, w_gate, w_up, w_down):
    n, d = xn.shape
    fe = w_gate.shape[-1]
    tm, tf = MOE_ROWS, MOE_COLS
    nf = fe // tf
    n_tiles = tile_expert.shape[0]
    col = lambda i, f, nu: jnp.where(i < nu[0], f, nf - 1)
    out = pl.pallas_call(
        _experts_kernel,
        grid_spec=pltpu.PrefetchScalarGridSpec(
            num_scalar_prefetch=4,
            grid=(n_tiles, nf),
            in_specs=[pl.BlockSpec(memory_space=pl.ANY),
                      pl.BlockSpec((1, 1, d, tf), lambda i, f, te, nu, s, t: (layer, te[i], 0, col(i, f, nu))),
                      pl.BlockSpec((1, 1, d, tf), lambda i, f, te, nu, s, t: (layer, te[i], 0, col(i, f, nu))),
                      pl.BlockSpec((1, 1, tf, d), lambda i, f, te, nu, s, t: (layer, te[i], col(i, f, nu), 0))],
            out_specs=pl.BlockSpec(memory_space=pl.ANY),
            scratch_shapes=[pltpu.VMEM((2, tm, d), F32), pltpu.VMEM((tm, d), BF16),
                            pltpu.VMEM((tm, d), F32), pltpu.VMEM((tm, d), F32),
                            pltpu.SemaphoreType.DMA((2,)), pltpu.SemaphoreType.DMA(())],
        ),
        out_shape=jax.ShapeDtypeStruct((2 * (n + tm), d), F32),
        compiler_params=_params("arbitrary", "arbitrary"),
        name="moe_experts",
    )(tile_expert, n_used, src, dst, xn, w_gate, w_up, w_down)
    return out.reshape(2, n + tm, d)


def _combine_kernel(h_ref, gate_ref, fn_ref, e0_ref, e1_ref, o_ref, *, final_norm):
    gates = gate_ref[...]
    out = h_ref[...] + gates[:, 0:1] * e0_ref[0] + gates[:, 1:2] * e1_ref[0]
    if final_norm:
        out = _rms(out, fn_ref[...])
    o_ref[...] = out


def _combine(h, gates, eo, final_gain):
    n, d = h.shape
    tm = FFN_ROWS
    final_norm = final_gain is not None
    fn = (final_gain if final_norm else jnp.ones((d,), F32)).reshape(1, d).astype(F32)
    rows = pl.BlockSpec((tm, d), lambda i: (i, 0))
    return pl.pallas_call(
        functools.partial(_combine_kernel, final_norm=final_norm),
        grid=(n // tm,),
        in_specs=[rows, pl.BlockSpec((tm, LANES), lambda i: (i, 0)), _full((1, d)),
                  pl.BlockSpec((1, tm, d), lambda i: (0, i, 0)),
                  pl.BlockSpec((1, tm, d), lambda i: (1, i, 0))],
        out_specs=rows,
        out_shape=jax.ShapeDtypeStruct((n, d), F32),
        compiler_params=_params("arbitrary"),
        name="moe_combine",
    )(h, gates, fn, eo, eo)


def _moe(h, gn, router, layer, w_gate, w_up, w_down, final_gain):
    n, d = h.shape
    xn, meta, gates, counts = _router(h, gn, router)
    n_tiles = (2 * n) // MOE_ROWS + N_EXPERTS
    cap = n_tiles * MOE_ROWS
    cnt = counts[0, :N_EXPERTS].astype(jnp.int32)
    padded = (cnt + MOE_ROWS - 1) // MOE_ROWS * MOE_ROWS
    pad_end = jnp.cumsum(padded)
    pad_start = pad_end - padded
    slot = (pad_start[meta[:, 0:2]] + meta[:, 2:4]).reshape(-1).astype(jnp.int32)
    tile_start = jnp.arange(n_tiles, dtype=jnp.int32) * MOE_ROWS
    tile_expert = jnp.minimum(jnp.sum(tile_start[:, None] >= pad_end[None, :], axis=1),
                              N_EXPERTS - 1).astype(jnp.int32)
    n_used = (pad_end[-1:] // MOE_ROWS).astype(jnp.int32)
    ids = jnp.arange(2 * n, dtype=jnp.int32)
    src = jnp.zeros((cap,), jnp.int32).at[slot].set(ids // 2, unique_indices=True)
    spare = n + jnp.arange(cap, dtype=jnp.int32) % MOE_ROWS
    dst = spare.at[slot].set((ids % 2) * (n + MOE_ROWS) + ids // 2, unique_indices=True)
    eo = _experts(xn, tile_expert, n_used, src, dst, layer, w_gate, w_up, w_down)
    return _combine(h, gates, eo, final_gain)


def kernel(x, norm_mix, norm_ffn, final_norm, rwkv_mu, rwkv_w_rkv, rwkv_w0, rwkv_w1, rwkv_w2, rwkv_a0, rwkv_a1, rwkv_a2, rwkv_g1, rwkv_g2, rwkv_k_k, rwkv_k_a, rwkv_r_k, rwkv_lnx_w, rwkv_lnx_b, rwkv_w_o, rwkv_v0, rwkv_v1, rwkv_v2, s5_lam_re, s5_lam_im, s5_log_step, s5_b_re, s5_b_im, s5_c_re, s5_c_im, s5_d, s5_w_glu, s5_b_glu, ffn_gate, ffn_up, ffn_down, moe_router, moe_gate, moe_up, moe_down):
    batch, seq, d = x.shape
    depth = norm_mix.shape[0]
    assert d % LANES == 0 and seq % WKV_CHUNK == 0
    n = batch * seq
    h = x.astype(F32).reshape(n, d)
    v_first = None
    for i in range(depth):
        j = i // 2
        last = i == depth - 1
        if i % 2 == 0:
            v_mix = None if j == 0 else (rwkv_v0[j - 1], rwkv_v1[j - 1], rwkv_v2[j - 1])
            r, lw, k, v, kk, b, g = _rwkv_proj(
                h, seq, norm_mix[i], rwkv_mu[j], rwkv_w_rkv[j], rwkv_w0[j], rwkv_w1[j], rwkv_w2[j],
                rwkv_a0[j], rwkv_a1[j], rwkv_a2[j], rwkv_g1[j], rwkv_g2[j], rwkv_k_k[j], rwkv_k_a[j],
                v_first, v_mix)
            if v_first is None:
                v_first = v
            yg = _wkv(r, lw, k, v, kk, b, g, rwkv_r_k[j], rwkv_lnx_w[j], rwkv_lnx_b[j], seq)
            h = _wo_ffn(h, yg, rwkv_w_o[j], norm_ffn[i], ffn_gate[j], ffn_up[j], ffn_down[j])
            if last:
                h = _final_norm(h, final_norm)
        else:
            h = _s5_glu(h, seq, norm_mix[i], s5_lam_re[j], s5_lam_im[j], s5_log_step[j], s5_b_re[j], s5_b_im[j],
                        s5_c_re[j], s5_c_im[j], s5_d[j], s5_w_glu[j], s5_b_glu[j])
            h = _moe(h, norm_ffn[i], moe_router[j], j, moe_gate, moe_up, moe_down,
                     final_norm if last else None)
    return h.reshape(batch, seq, d).astype(x.dtype)


def _final_norm_kernel(h_ref, g_ref, o_ref):
    o_ref[...] = _rms(h_ref[...], g_ref[...])


def _final_norm(h, gain):
    n, d = h.shape
    rows = pl.BlockSpec((FFN_ROWS, d), lambda i: (i, 0))
    return pl.pallas_call(
        _final_norm_kernel, grid=(n // FFN_ROWS,), in_specs=[rows, _full((1, d))], out_specs=rows,
        out_shape=jax.ShapeDtypeStruct((n, d), F32), compiler_params=_params("arbitrary"),
        name="final_norm",
    )(h, gain.reshape(1, d).astype(F32))
```

```python
import functools
import math

import jax
import jax.numpy as jnp
from jax import lax
from jax.experimental import pallas as pl
from jax.experimental.pallas import tpu as pltpu

F32 = jnp.float32
BF16 = jnp.bfloat16

NORM_EPS = 1e-6
LNX_EPS = 64e-5
HEAD_DIM = 64
S5_GROUP = 16
S5_STATE = 64
S5_EIG_MAX = -1e-4
N_EXPERTS = 8
LANES = 128
SUBLANES = 8
WKV_CHUNK = 64
WKV_BATCH_ROWS = 2
VMEM_LIMIT = 56 * 1024 * 1024

PROJ_ROWS = 256
FFN_ROWS = 512
FFN_COLS = 1408
S5_ROWS = 512
ROUTER_ROWS = 512
MOE_ROWS = 768
MOE_COLS = 896
NEG_BIG = -1e30


def _params(*semantics):
    return pltpu.CompilerParams(dimension_semantics=semantics, vmem_limit_bytes=VMEM_LIMIT)


def _dot(a, b):
    return jnp.dot(a.astype(BF16), b.astype(BF16), preferred_element_type=F32)


def _dot_nt(a, b):
    return lax.dot_general(a.astype(BF16), b.astype(BF16), (((1,), (1,)), ((), ())),
                           preferred_element_type=F32)


def _dot_tn(a, b):
    return lax.dot_general(a.astype(BF16), b.astype(BF16), (((0,), (0,)), ((), ())),
                           preferred_element_type=F32)


def _split2(x):
    hi = x.astype(BF16)
    lo = (x - hi.astype(F32)).astype(BF16)
    return hi, lo


def _split3(x):
    hi = x.astype(BF16)
    r1 = x - hi.astype(F32)
    mid = r1.astype(BF16)
    lo = (r1 - mid.astype(F32)).astype(BF16)
    return hi, mid, lo


def _rms(x, g):
    return x * lax.rsqrt(jnp.mean(x * x, axis=-1, keepdims=True) + NORM_EPS) * g


def _sigmoid(x):
    return 1.0 / (1.0 + jnp.exp(-x))


def _full(shape):
    n = len(shape)
    return pl.BlockSpec(shape, lambda *_: (0,) * n)


def _rwkv_proj_kernel(*refs, has_vmix, tiles_per_seq):
    (x_ref, xp_ref, gn_ref, mu_ref, wrkv_ref, w0_ref, w1_ref, w2_ref, a0_ref, a1_ref, a2_ref,
     g1_ref, g2_ref, kk_ref, ka_ref, hs_ref, hst_ref) = refs[:17]
    rest = refs[17:]
    if has_vmix:
        vf_ref, v0_ref, v1_ref, v2_ref = rest[:4]
        rest = rest[4:]
    r_o, lw_o, k_o, v_o, kk_o, b_o, g_o = rest

    i = pl.program_id(0)
    gn = gn_ref[...]
    hn = _rms(x_ref[...], gn)
    starts_seq = (i % tiles_per_seq) == 0
    prev = _rms(xp_ref[SUBLANES - 1:SUBLANES, :], gn) * jnp.where(starts_seq, 0.0, 1.0)
    first_row = lax.broadcasted_iota(jnp.int32, hn.shape, 0) == 0
    shifted = jnp.where(first_row, prev, pltpu.roll(hn, 1, axis=0))
    xx = shifted - hn
    xr, xk, xv, xw, xa, xg = [(hn + xx * mu_ref[s:s + 1, :]).astype(BF16) for s in range(6)]

    r = jnp.dot(xr, wrkv_ref[0], preferred_element_type=F32)
    k = jnp.dot(xk, wrkv_ref[1], preferred_element_type=F32)
    v = jnp.dot(xv, wrkv_ref[2], preferred_element_type=F32)

    z = w0_ref[...] + _dot(jnp.tanh(_dot(xw, w1_ref[...])), w2_ref[...])
    lw_o[...] = -math.exp(-0.5) * _sigmoid(z)

    if has_vmix:
        mixv = _sigmoid(v0_ref[...] + _dot(_dot(xv, v1_ref[...]), v2_ref[...]))
        v = v + (vf_ref[...].astype(F32) - v) * mixv
    a = _sigmoid(a0_ref[...] + _dot(_dot(xa, a1_ref[...]), a2_ref[...]))
    g = _dot(_sigmoid(_dot(xg, g1_ref[...])), g2_ref[...])

    kk = k * kk_ref[...]
    hi, lo = _split2(kk * kk)
    ss = (jnp.dot(hi, hs_ref[...], preferred_element_type=F32)
          + jnp.dot(lo, hs_ref[...], preferred_element_type=F32))
    inv = 1.0 / jnp.maximum(jnp.sqrt(ss), 1e-12)
    hi, lo = _split2(inv)
    inv_full = (jnp.dot(hi, hst_ref[...], preferred_element_type=F32)
                + jnp.dot(lo, hst_ref[...], preferred_element_type=F32))
    kk = kk * inv_full
    k = k * (1.0 + (a - 1.0) * ka_ref[...])

    r_o[...] = r.astype(BF16)
    k_o[...] = k.astype(BF16)
    v_o[...] = v.astype(BF16)
    kk_o[...] = kk.astype(BF16)
    b_o[...] = (kk * a).astype(BF16)
    g_o[...] = g.astype(BF16)


def _pad_cols(w, n):
    return jnp.pad(w, ((0, 0), (0, n - w.shape[1])))


def _pad_rows(w, n):
    return jnp.pad(w, ((0, n - w.shape[0]), (0, 0)))


def _rwkv_proj(h, seq, gn, mu, w_rkv, w0, w1, w2, a0, a1, a2, g1, g2, k_k, k_a, v_first, v_mix):
    n, d = h.shape
    row2 = lambda p: p.reshape(1, d).astype(F32)
    lora = lambda down, up: (_pad_cols(down, _rup(down.shape[1])).astype(BF16),
                             _pad_rows(up, _rup(up.shape[0])).astype(BF16))
    w1p, w2p = lora(w1, w2)
    a1p, a2p = lora(a1, a2)
    g1p, g2p = lora(g1, g2)
    head_of = jnp.arange(d, dtype=jnp.int32) // HEAD_DIM
    hs = (head_of[:, None] == jnp.arange(LANES, dtype=jnp.int32)[None, :]).astype(BF16)
    hst = hs.T
    has_vmix = v_mix is not None
    tm = PROJ_ROWS
    assert seq % tm == 0
    row_spec = pl.BlockSpec((tm, d), lambda i: (i, 0))
    prev_spec = pl.BlockSpec((SUBLANES, d), lambda i: (jnp.maximum(i * (tm // SUBLANES) - 1, 0), 0))
    args = [h, h, row2(gn), mu.astype(F32), w_rkv.astype(BF16), row2(w0), w1p, w2p, row2(a0), a1p, a2p,
            g1p, g2p, row2(k_k), row2(k_a), hs, hst]
    specs = [row_spec, prev_spec] + [_full(a.shape) for a in args[2:]]
    if has_vmix:
        v0, v1, v2 = v_mix
        v1p, v2p = lora(v1, v2)
        extra = [v_first, row2(v0), v1p, v2p]
        args += extra
        specs += [row_spec] + [_full(a.shape) for a in extra[1:]]
    bf = jax.ShapeDtypeStruct((n, d), BF16)
    out_shape = [bf, jax.ShapeDtypeStruct((n, d), F32), bf, bf, bf, bf, bf]
    return pl.pallas_call(
        functools.partial(_rwkv_proj_kernel, has_vmix=has_vmix, tiles_per_seq=seq // tm),
        grid=(n // tm,),
        in_specs=specs,
        out_specs=[row_spec] * 7,
        out_shape=out_shape,
        compiler_params=_params("arbitrary"),
        name="rwkv_proj",
    )(*args)


def _rup(x, m=LANES):
    return (x + m - 1) // m * m


def _unit_lower_inverses(mats, row, col):
    same = lambda nb: (row // nb) == (col // nb)
    eye = jnp.where(row == col, 1.0, 0.0)
    n1 = [jnp.where(same(8), a, 0.0) for a in mats]
    n2 = [_dot(x, x) for x in n1]
    n4 = [_dot(x, x) for x in n2]
    t = [eye + x for x in n1]
    t = [ti + _dot(ti, x) for ti, x in zip(t, n2)]
    t = [ti + _dot(ti, x) for ti, x in zip(t, n4)]
    size = 8
    while size < WKV_CHUNK:
        grow = jnp.logical_and(same(2 * size), jnp.logical_not(same(size)))
        left = [_dot(ti, jnp.where(grow, a, 0.0)) for ti, a in zip(t, mats)]
        t = [ti + _dot(li, ti) for ti, li in zip(t, left)]
        size *= 2
    return t


def _wkv_kernel(r_ref, lw_ref, k_ref, v_ref, kk_ref, b_ref, g_ref, rk_ref, lnw_ref, lnb_ref,
                o_ref, s_ref):
    @pl.when(pl.program_id(1) == 0)
    def _():
        s_ref[...] = jnp.zeros_like(s_ref)

    nb, chunk, d = r_ref.shape
    pairs = d // LANES
    pr = 2 * chunk
    row = lax.broadcasted_iota(jnp.int32, (pr, pr), 0)
    col = lax.broadcasted_iota(jnp.int32, (pr, pr), 1)
    same_head = (row // chunk) == (col // chunk)
    incl = jnp.logical_and(same_head, row >= col)
    incl2 = jnp.concatenate([incl, incl], axis=1)
    strict = jnp.logical_and(same_head, row > col)
    crow = lax.broadcasted_iota(jnp.int32, (chunk, chunk), 0)
    ccol = lax.broadcasted_iota(jnp.int32, (chunk, chunk), 1)
    tri = jnp.where(crow >= ccol, 1.0, 0.0).astype(BF16)
    lane = lambda p: slice(p * LANES, (p + 1) * LANES)

    def block_diag(x, p):
        xp = x[:, lane(p)]
        return jnp.where(same_head, jnp.concatenate([xp, xp], axis=0), 0.0).astype(BF16)

    def stack_heads(x):
        return jnp.concatenate([x[:, :HEAD_DIM], x[:, HEAD_DIM:]], axis=0)

    ar, bk, bke, vs, p_end = [], [], [], [], []
    for bi in range(nb):
        lw = lw_ref[bi]
        lp = sum(jnp.dot(tri, part, preferred_element_type=F32) for part in _split3(lw))
        lp_end = lp[chunk - 1:chunk, :]
        r = r_ref[bi].astype(F32)
        k = k_ref[bi].astype(F32)
        kk = kk_ref[bi].astype(F32)
        b = b_ref[bi].astype(F32)
        p_inv = jnp.exp(-lp)
        to_end = jnp.exp(lp_end - lp)
        a_t = -kk * jnp.exp(lp - lw)
        b_t = b * p_inv
        k_t = k * p_inv
        r_t = r * jnp.exp(lp)
        b_e = b * to_end
        k_e = k * to_end
        v_b = v_ref[bi]
        for p in range(pairs):
            ar.append(jnp.concatenate([block_diag(a_t, p), block_diag(r_t, p)], axis=0))
            bk.append(jnp.concatenate([block_diag(b_t, p), block_diag(k_t, p)], axis=0))
            bke.append(jnp.concatenate([block_diag(b_e, p), block_diag(k_e, p)], axis=0))
            vs.append(stack_heads(v_b[:, lane(p)]))
            p_end.append(jnp.exp(lp_end[:, lane(p)]))

    qs = range(nb * pairs)
    s = [s_ref[q] for q in qs]
    gram = [_dot_nt(ar[q], bk[q]) for q in qs]
    ars = [_dot_nt(ar[q], s[q]) for q in qs]
    a_ab = [jnp.where(strict, gram[q][:pr, :pr], 0.0) for q in qs]
    a_ak = [jnp.where(strict, gram[q][:pr, pr:], 0.0) for q in qs]
    r_bk = [jnp.where(incl2, gram[q][pr:, :], 0.0).astype(BF16) for q in qs]
    w_in = [ars[q][:pr] + _dot(a_ak[q], vs[q]) for q in qs]
    t_inv = _unit_lower_inverses(a_ab, row, col)
    u = [_dot(t_inv[q], w_in[q]) for q in qs]
    uv = [jnp.concatenate([u[q].astype(BF16), vs[q]], axis=0) for q in qs]
    y = [ars[q][pr:] + jnp.dot(r_bk[q], uv[q], preferred_element_type=F32) for q in qs]
    for q in qs:
        s_ref[q] = s[q] * p_end[q] + _dot_tn(uv[q], bke[q])

    def head_norm(yh):
        mean = jnp.mean(yh, axis=-1, keepdims=True)
        yc = yh - mean
        var = jnp.mean(yc * yc, axis=-1, keepdims=True)
        return yc * lax.rsqrt(var + LNX_EPS)

    left = lax.broadcasted_iota(jnp.int32, (chunk, LANES), 1) < HEAD_DIM
    for bi in range(nb):
        rkk = r_ref[bi].astype(F32) * k_ref[bi].astype(F32) * rk_ref[...]
        yn, dots = [], []
        for p in range(pairs):
            yq = y[bi * pairs + p]
            yn.append(jnp.concatenate([head_norm(yq[:chunk]), head_norm(yq[chunk:])], axis=-1))
            rk_p = rkk[:, lane(p)]
            dots.append(jnp.where(left, jnp.sum(rk_p[:, :HEAD_DIM], axis=-1, keepdims=True),
                                  jnp.sum(rk_p[:, HEAD_DIM:], axis=-1, keepdims=True)))
        yn = jnp.concatenate(yn, axis=-1)
        bonus = jnp.concatenate(dots, axis=-1) * v_ref[bi].astype(F32)
        o_ref[bi] = ((yn * lnw_ref[...] + lnb_ref[...] + bonus) * g_ref[bi].astype(F32)).astype(BF16)


def _wkv(r, lw, k, v, kk, b, g, r_k, lnx_w, lnx_b, seq):
    n, d = r.shape
    batch = n // seq
    nb = WKV_BATCH_ROWS
    assert batch % nb == 0
    view = lambda a: a.reshape(batch, seq, d)
    blk = pl.BlockSpec((nb, WKV_CHUNK, d), lambda bi, c: (bi, c, 0))
    vec = pl.BlockSpec((1, d), lambda bi, c: (0, 0))
    row2 = lambda p: p.reshape(1, d).astype(F32)
    out = pl.pallas_call(
        _wkv_kernel,
        grid=(batch // nb, seq // WKV_CHUNK),
        in_specs=[blk] * 7 + [vec] * 3,
        out_specs=blk,
        out_shape=jax.ShapeDtypeStruct((batch, seq, d), BF16),
        scratch_shapes=[pltpu.VMEM((nb * (d // LANES), HEAD_DIM, LANES), F32)],
        compiler_params=_params("arbitrary", "arbitrary"),
        name="wkv7",
    )(view(r), view(lw), view(k), view(v), view(kk), view(b), view(g), row2(r_k), row2(lnx_w), row2(lnx_b))
    return out.reshape(n, d)


def _wo_ffn_kernel(h_ref, yg_ref, wo_ref, gn_ref, wg_ref, wu_ref, wd_ref, o_ref, h1_s, hn_s, acc_s):
    f = pl.program_id(1)

    @pl.when(f == 0)
    def _():
        h1 = h_ref[...] + jnp.dot(yg_ref[...], wo_ref[...], preferred_element_type=F32)
        h1_s[...] = h1
        hn_s[...] = _rms(h1, gn_ref[...]).astype(BF16)
        acc_s[...] = jnp.zeros_like(acc_s)

    x = hn_s[...]
    gate = jnp.dot(x, wg_ref[...], preferred_element_type=F32)
    up = jnp.dot(x, wu_ref[...], preferred_element_type=F32)
    act = (gate * _sigmoid(gate) * up).astype(BF16)
    acc_s[...] += jnp.dot(act, wd_ref[...], preferred_element_type=F32)

    @pl.when(f == pl.num_programs(1) - 1)
    def _():
        o_ref[...] = h1_s[...] + acc_s[...]


def _wo_ffn(h, yg, w_o, gn, w_gate, w_up, w_down):
    n, d = h.shape
    ff = w_gate.shape[1]
    tm, tf = FFN_ROWS, FFN_COLS
    rows = pl.BlockSpec((tm, d), lambda i, f: (i, 0))
    return pl.pallas_call(
        _wo_ffn_kernel,
        grid=(n // tm, ff // tf),
        in_specs=[rows, rows, _full((d, d)), _full((1, d)),
                  pl.BlockSpec((d, tf), lambda i, f: (0, f)),
                  pl.BlockSpec((d, tf), lambda i, f: (0, f)),
                  pl.BlockSpec((tf, d), lambda i, f: (f, 0))],
        out_specs=rows,
        out_shape=jax.ShapeDtypeStruct((n, d), F32),
        scratch_shapes=[pltpu.VMEM((tm, d), F32), pltpu.VMEM((tm, d), BF16), pltpu.VMEM((tm, d), F32)],
        compiler_params=_params("arbitrary", "arbitrary"),
        name="wo_ffn",
    )(h, yg, w_o.astype(BF16), gn.reshape(1, d).astype(F32),
      w_gate.astype(BF16), w_up.astype(BF16), w_down.astype(BF16))


def _s5_disc_kernel(lr_ref, li_ref, dt_ref, bre_ref, bim_ref, are_o, aim_o, bbre_o, bbim_o):
    lr = jnp.minimum(lr_ref[...], S5_EIG_MAX)
    li = li_ref[...]
    dt = dt_ref[...]
    dr, di = lr * dt, li * dt
    e = jnp.exp(dr)
    abar_re = e * jnp.cos(di)
    abar_im = e * jnp.sin(di)
    sh = jnp.sin(0.5 * di)
    num_re = (e - 1.0) * jnp.cos(di) - 2.0 * sh * sh
    den = lr * lr + li * li
    coef_re = (num_re * lr + abar_im * li) / den
    coef_im = (abar_im * lr - num_re * li) / den
    are_o[...] = abar_re
    aim_o[...] = abar_im
    bre, bim = bre_ref[...], bim_ref[...]
    bbre_o[...] = coef_re * bre - coef_im * bim
    bbim_o[...] = coef_re * bim + coef_im * bre


def _s5_discretise(lam_re, lam_im, log_step, b_re, b_im):
    groups, states = lam_re.shape
    q = b_re.shape[-1]
    gp = groups * states
    col = lambda a: a.reshape(gp, 1).astype(F32)
    dt = jnp.exp(log_step.astype(F32))
    dt_col = jnp.broadcast_to(dt[:, None], (groups, states)).reshape(gp, 1)
    one = jax.ShapeDtypeStruct((gp, 1), F32)
    wide = jax.ShapeDtypeStruct((gp, q), F32)
    return pl.pallas_call(
        _s5_disc_kernel,
        out_shape=[one, one, wide, wide],
        name="s5_discretise",
    )(col(lam_re), col(lam_im), dt_col, b_re.reshape(gp, q).astype(F32), b_im.reshape(gp, q).astype(F32))


def _s5_glu_kernel(h_ref, gn_ref, wb_ref, are_ref, aim_ref, wcre_ref, wcim_ref, dsk_ref, wglu_ref, bglu_ref,
                   o_ref, st_s, xt_s, bu_s, y_s):
    @pl.when(pl.program_id(0) == 0)
    def _():
        st_s[...] = jnp.zeros_like(st_s)

    batch, steps, d = h_ref.shape
    slabs = d // LANES
    half = st_s.shape[-1] // 2
    lane = lambda s: slice(s * LANES, (s + 1) * LANES)
    gn = gn_ref[...]
    dsk = dsk_ref[...]

    for b in range(batch):
        hn_b = _rms(h_ref[b], gn)
        for s in range(slabs):
            xt_s[s, pl.ds(b, steps, stride=batch), :] = hn_b[:, lane(s)]

    for s in range(slabs):
        sl = lane(s)
        xs = xt_s[s]
        bu_s[...] = jnp.dot(xs.astype(BF16), wb_ref[s], preferred_element_type=F32)
        a_re = are_ref[s]
        a_im = aim_ref[s]

        def step(t, carry):
            h_re, h_im = carry
            rows = pl.ds(pl.multiple_of(t * batch, batch), batch)
            n_re = a_re * h_re - a_im * h_im + bu_s[rows, :half]
            n_im = a_re * h_im + a_im * h_re + bu_s[rows, half:]
            bu_s[rows, :half] = n_re
            bu_s[rows, half:] = n_im
            return n_re, n_im

        h_re, h_im = lax.fori_loop(0, steps, step, (st_s[s, :, :half], st_s[s, :, half:]), unroll=8)
        st_s[s, :, :half] = h_re
        st_s[s, :, half:] = h_im
        y = (jnp.dot(bu_s[:, :half].astype(BF16), wcre_ref[s], preferred_element_type=F32)
             - jnp.dot(bu_s[:, half:].astype(BF16), wcim_ref[s], preferred_element_type=F32))
        y = y + dsk[:, sl] * xs
        y = 0.5 * y * (1.0 + jnp.tanh(math.sqrt(2.0 / math.pi) * (y + 0.044715 * (y * y * y))))
        y_s[:, sl] = y.astype(BF16)

    z = jnp.dot(y_s[...], wglu_ref[...], preferred_element_type=F32) + bglu_ref[...]
    glu = z[:, :d] * _sigmoid(z[:, d:])
    for s in range(slabs):
        xt_s[s] = glu[:, lane(s)]
    for b in range(batch):
        for s in range(slabs):
            o_ref[b, :, lane(s)] = h_ref[b, :, lane(s)] + xt_s[s, pl.ds(b, steps, stride=batch), :]


def _s5_glu(h, seq, gn, lam_re, lam_im, log_step, b_re, b_im, c_re, c_im, d_skip, w_glu, b_glu):
    n, d = h.shape
    groups, states = lam_re.shape
    q = S5_GROUP
    slabs = d // LANES
    gl = LANES // q
    half = gl * states
    abar_re, abar_im, bbar_re, bbar_im = _s5_discretise(lam_re, lam_im, log_step, b_re, b_im)
    eye = jnp.eye(gl, dtype=F32)

    def in_proj(bb):
        bb = bb.reshape(slabs, gl, states, q)
        return jnp.einsum('sgpc,gh->sgchp', bb, eye).reshape(slabs, LANES, half)

    def out_proj(c):
        c = c.astype(F32).reshape(slabs, gl, q, states)
        return jnp.einsum('sgcp,gh->sgphc', c, eye).reshape(slabs, half, LANES).astype(BF16)

    wb = jnp.concatenate([in_proj(bbar_re), in_proj(bbar_im)], axis=-1).astype(BF16)
    lanes_of = lambda a: jnp.broadcast_to(a.reshape(slabs, 1, half), (slabs, SUBLANES, half))
    batch = n // seq
    assert batch == SUBLANES
    steps = S5_ROWS // batch
    tm = S5_ROWS
    rows = pl.BlockSpec((batch, steps, d), lambda i: (0, i, 0))
    args = [h.reshape(batch, seq, d), gn.reshape(1, d).astype(F32), wb, lanes_of(abar_re), lanes_of(abar_im),
            out_proj(c_re), out_proj(c_im), d_skip.reshape(1, d).astype(F32),
            w_glu.astype(BF16), b_glu.reshape(1, 2 * d).astype(F32)]
    out = pl.pallas_call(
        _s5_glu_kernel,
        grid=(seq // steps,),
        in_specs=[rows] + [_full(a.shape) for a in args[1:]],
        out_specs=rows,
        out_shape=jax.ShapeDtypeStruct((batch, seq, d), F32),
        scratch_shapes=[pltpu.VMEM((slabs, batch, 2 * half), F32),
                        pltpu.VMEM((slabs, tm, LANES), F32),
                        pltpu.VMEM((tm, 2 * half), F32),
                        pltpu.VMEM((tm, d), BF16)],
        compiler_params=_params("arbitrary"),
        name="s5_glu",
    )(*args)
    return out.reshape(n, d)


def _router_kernel(h_ref, gn_ref, rhi_ref, rlo_ref, xn_o, meta_o, gate_o, cnt_o, base_s):
    @pl.when(pl.program_id(0) == 0)
    def _():
        base_s[...] = jnp.zeros_like(base_s)

    tm = h_ref.shape[0]
    hn = _rms(h_ref[...], gn_ref[...])
    xn_o[...] = hn
    hi, lo = _split2(hn)
    logits = (jnp.dot(hi, rhi_ref[...], preferred_element_type=F32)
              + jnp.dot(hi, rlo_ref[...], preferred_element_type=F32)
              + jnp.dot(lo, rhi_ref[...], preferred_element_type=F32))
    lane = lax.broadcasted_iota(jnp.int32, (tm, LANES), 1).astype(F32)
    logits = jnp.where(lane < N_EXPERTS, logits, NEG_BIG)
    m1 = jnp.max(logits, axis=-1, keepdims=True)
    i1 = jnp.min(jnp.where(logits == m1, lane, float(LANES)), axis=-1, keepdims=True)
    rest = jnp.where(lane == i1, NEG_BIG, logits)
    m2 = jnp.max(rest, axis=-1, keepdims=True)
    i2 = jnp.min(jnp.where(rest == m2, lane, float(LANES)), axis=-1, keepdims=True)
    e = jnp.exp(m2 - m1)
    g1 = 1.0 / (1.0 + e)
    g2 = e / (1.0 + e)

    oh1 = jnp.where(lane == i1, 1.0, 0.0)
    oh2 = jnp.where(lane == i2, 1.0, 0.0)
    both = oh1 + oh2
    row = lax.broadcasted_iota(jnp.int32, (tm, tm), 0)
    col = lax.broadcasted_iota(jnp.int32, (tm, tm), 1)
    before = jnp.where(row > col, 1.0, 0.0).astype(BF16)
    seen = jnp.dot(before, both.astype(BF16), preferred_element_type=F32) + base_s[...]
    rank1 = jnp.sum(oh1 * seen, axis=-1, keepdims=True)
    rank2 = jnp.sum(oh2 * seen, axis=-1, keepdims=True)
    base = base_s[...] + jnp.sum(both, axis=0, keepdims=True)
    base_s[...] = base
    cnt_o[...] = jnp.broadcast_to(base, cnt_o.shape)

    meta = jnp.where(lane == 0.0, i1, jnp.where(lane == 1.0, i2, jnp.where(lane == 2.0, rank1, rank2)))
    meta_o[...] = meta.astype(jnp.int32)
    gate_o[...] = jnp.where(lane == 0.0, g1, g2)


def _router(h, gn, router):
    n, d = h.shape
    tm = ROUTER_ROWS
    rp = _pad_cols(router.astype(F32), LANES)
    rhi = rp.astype(BF16)
    rlo = (rp - rhi.astype(F32)).astype(BF16)
    rows = pl.BlockSpec((tm, d), lambda i: (i, 0))
    narrow = pl.BlockSpec((tm, LANES), lambda i: (i, 0))
    return pl.pallas_call(
        _router_kernel,
        grid=(n // tm,),
        in_specs=[rows, _full((1, d)), _full((d, LANES)), _full((d, LANES))],
        out_specs=[rows, narrow, narrow, _full((SUBLANES, LANES))],
        out_shape=[jax.ShapeDtypeStruct((n, d), F32),
                   jax.ShapeDtypeStruct((n, LANES), jnp.int32),
                   jax.ShapeDtypeStruct((n, LANES), F32),
                   jax.ShapeDtypeStruct((SUBLANES, LANES), F32)],
        scratch_shapes=[pltpu.VMEM((1, LANES), F32)],
        compiler_params=_params("arbitrary"),
        name="moe_router",
    )(h, gn.reshape(1, d).astype(F32), rhi, rlo)


def _row_copy(src_ref, src_row, dst_ref, dst_row, sem):
    return pltpu.make_async_copy(src_ref.at[pl.ds(src_row, 1)], dst_ref.at[pl.ds(dst_row, 1)], sem)


def _slot_maps_kernel(slot_ref, seg_ref, src_o, dst_o, *, n, tm):
    total = src_o.shape[0]

    def spare(s):
        return 2 * n + ((s // tm) % 3) * tm + s % tm

    def fill(first, last):
        def body(s, carry):
            src_o[s] = 0
            dst_o[tm + s] = spare(s)
            return carry
        lax.fori_loop(first, last, body, 0)

    for e in range(N_EXPERTS):
        fill(seg_ref[e] + seg_ref[N_EXPERTS + e], seg_ref[2 * N_EXPERTS + e])
    fill(seg_ref[3 * N_EXPERTS - 1], total - tm)

    def lead(s, carry):
        dst_o[s] = spare(s + 2 * tm)
        src_o[total - tm + s] = 0
        return carry
    lax.fori_loop(0, tm, lead, 0)

    def place(a, carry):
        s = slot_ref[a]
        src_o[s] = a // 2
        dst_o[tm + s] = (a % 2) * n + a // 2
        return carry
    lax.fori_loop(0, 2 * n, place, 0, unroll=8)


def _slot_maps(slot, segments, n, n_tiles):
    tm = MOE_ROWS
    total = (n_tiles + 1) * tm
    smem = pl.BlockSpec(memory_space=pltpu.SMEM)
    return pl.pallas_call(
        functools.partial(_slot_maps_kernel, n=n, tm=tm),
        in_specs=[smem, smem],
        out_specs=[smem, smem],
        out_shape=[jax.ShapeDtypeStruct((total,), jnp.int32)] * 2,
        name="moe_slot_maps",
    )(slot, segments)


def _experts_kernel(te_ref, nu_ref, src_ref, dst_ref, xn_ref, wg_ref, wu_ref, wd_ref, out_ref,
                    xg_s, xb_s, acc_s, ob_s, gsem, osem, *, n_cols):
    del te_ref
    i = pl.program_id(0)
    f = pl.program_id(1)
    n_used = nu_ref[0]
    used = i < n_used
    tm = acc_s.shape[0]
    per_step = tm // n_cols
    slot = i % 2
    other = 1 - slot

    def gather_row(tile, buf, r):
        _row_copy(xn_ref, src_ref[tile * tm + r], xg_s.at[buf], r, gsem.at[buf]).start()

    def scatter_row(tile, buf, r):
        _row_copy(ob_s.at[buf], r, out_ref, dst_ref[(tile + 1) * tm + r], osem.at[buf]).start()

    def wait_gather(buf):
        pltpu.make_async_copy(xn_ref.at[pl.ds(0, tm)], xg_s.at[buf], gsem.at[buf]).wait()

    def wait_scatter(buf):
        pltpu.make_async_copy(ob_s.at[buf], out_ref.at[pl.ds(0, tm)], osem.at[buf]).wait()

    @pl.when(jnp.logical_and(used, f == 0))
    def _():
        @pl.when(i == 0)
        def _():
            def body(r, carry):
                gather_row(0, 0, r)
                return carry
            lax.fori_loop(0, tm, body, 0, unroll=8)
            ob_s[...] = jnp.zeros_like(ob_s)
            spare0 = out_ref.shape[0] - 3 * tm
            for k in range(3):
                pltpu.make_async_copy(ob_s.at[0], out_ref.at[pl.ds(spare0 + k * tm, tm)], osem.at[0]).start()
            for k in range(3):
                wait_scatter(0)

        wait_gather(slot)
        xb_s[...] = xg_s[slot].astype(BF16)
        acc_s[...] = jnp.zeros_like(acc_s)

    @pl.when(used)
    def _():
        for u in range(per_step):
            r = f * per_step + u
            gather_row(i + 1, other, r)
            scatter_row(i - 1, other, r)
        x = xb_s[...]
        gate = jnp.dot(x, wg_ref[0, 0].astype(BF16), preferred_element_type=F32)
        up = jnp.dot(x, wu_ref[0, 0].astype(BF16), preferred_element_type=F32)
        act = (gate * _sigmoid(gate) * up).astype(BF16)
        acc_s[...] += jnp.dot(act, wd_ref[0, 0].astype(BF16), preferred_element_type=F32)

    @pl.when(jnp.logical_and(used, f == n_cols - 1))
    def _():
        @pl.when(i > 0)
        def _():
            wait_scatter(slot)

        ob_s[slot] = acc_s[...]

        @pl.when(i == n_used - 1)
        def _():
            def body(r, carry):
                scatter_row(i, slot, r)
                return carry
            lax.fori_loop(0, tm, body, 0, unroll=8)
            wait_scatter(other)
            wait_scatter(slot)
            wait_gather(other)


def _experts(xn, tile_expert, n_used, src, dst, layer, w_gate, w_up, w_down):
    n, d = xn.shape
    fe = w_gate.shape[-1]
    tm, tf = MOE_ROWS, MOE_COLS
    nf = fe // tf
    assert tm % nf == 0
    n_tiles = tile_expert.shape[0]
    col = lambda i, f, nu: jnp.where(i < nu[0], f, nf - 1)
    return pl.pallas_call(
        functools.partial(_experts_kernel, n_cols=nf),
        grid_spec=pltpu.PrefetchScalarGridSpec(
            num_scalar_prefetch=4,
            grid=(n_tiles, nf),
            in_specs=[pl.BlockSpec(memory_space=pl.ANY),
                      pl.BlockSpec((1, 1, d, tf), lambda i, f, te, nu, s, t: (layer, te[i], 0, col(i, f, nu))),
                      pl.BlockSpec((1, 1, d, tf), lambda i, f, te, nu, s, t: (layer, te[i], 0, col(i, f, nu))),
                      pl.BlockSpec((1, 1, tf, d), lambda i, f, te, nu, s, t: (layer, te[i], col(i, f, nu), 0))],
            out_specs=pl.BlockSpec(memory_space=pl.ANY),
            scratch_shapes=[pltpu.VMEM((2, tm, d), F32), pltpu.VMEM((tm, d), BF16),
                            pltpu.VMEM((tm, d), F32), pltpu.VMEM((2, tm, d), F32),
                            pltpu.SemaphoreType.DMA((2,)), pltpu.SemaphoreType.DMA((2,))],
        ),
        out_shape=jax.ShapeDtypeStruct((2 * n + 3 * tm, d), F32),
        compiler_params=_params("arbitrary", "arbitrary"),
        name="moe_experts",
    )(tile_expert, n_used, src, dst, xn, w_gate, w_up, w_down)


def _combine_kernel(h_ref, gate_ref, fn_ref, e0_ref, e1_ref, o_ref, *, final_norm):
    gates = gate_ref[...]
    out = h_ref[...] + gates[:, 0:1] * e0_ref[...] + gates[:, 1:2] * e1_ref[...]
    if final_norm:
        out = _rms(out, fn_ref[...])
    o_ref[...] = out


def _combine(h, gates, eo, final_gain):
    n, d = h.shape
    tm = FFN_ROWS
    final_norm = final_gain is not None
    fn = (final_gain if final_norm else jnp.ones((d,), F32)).reshape(1, d).astype(F32)
    rows = pl.BlockSpec((tm, d), lambda i: (i, 0))
    return pl.pallas_call(
        functools.partial(_combine_kernel, final_norm=final_norm),
        grid=(n // tm,),
        in_specs=[rows, pl.BlockSpec((tm, LANES), lambda i: (i, 0)), _full((1, d)),
                  rows, pl.BlockSpec((tm, d), lambda i: (i + n // tm, 0))],
        out_specs=rows,
        out_shape=jax.ShapeDtypeStruct((n, d), F32),
        compiler_params=_params("arbitrary"),
        name="moe_combine",
    )(h, gates, fn, eo, eo)


def _moe(h, gn, router, layer, w_gate, w_up, w_down, final_gain):
    n, d = h.shape
    tm = MOE_ROWS
    xn, meta, gates, counts = _router(h, gn, router)
    n_tiles = -(-2 * n // tm) + N_EXPERTS
    cnt = counts[0, :N_EXPERTS].astype(jnp.int32)
    padded = (cnt + tm - 1) // tm * tm
    pad_end = jnp.cumsum(padded)
    pad_start = pad_end - padded
    slot = (pad_start[meta[:, 0:2]] + meta[:, 2:4]).reshape(-1).astype(jnp.int32)
    tile_start = jnp.arange(n_tiles, dtype=jnp.int32) * tm
    tile_expert = jnp.minimum(jnp.sum(tile_start[:, None] >= pad_end[None, :], axis=1),
                              N_EXPERTS - 1).astype(jnp.int32)
    n_used = (pad_end[-1:] // tm).astype(jnp.int32)
    src, dst = _slot_maps(slot, jnp.concatenate([pad_start, cnt, pad_end]).astype(jnp.int32), n, n_tiles)
    eo = _experts(xn, tile_expert, n_used, src, dst, layer, w_gate, w_up, w_down)
    return _combine(h, gates, eo, final_gain)


def kernel(x, norm_mix, norm_ffn, final_norm, rwkv_mu, rwkv_w_rkv, rwkv_w0, rwkv_w1, rwkv_w2, rwkv_a0, rwkv_a1, rwkv_a2, rwkv_g1, rwkv_g2, rwkv_k_k, rwkv_k_a, rwkv_r_k, rwkv_lnx_w, rwkv_lnx_b, rwkv_w_o, rwkv_v0, rwkv_v1, rwkv_v2, s5_lam_re, s5_lam_im, s5_log_step, s5_b_re, s5_b_im, s5_c_re, s5_c_im, s5_d, s5_w_glu, s5_b_glu, ffn_gate, ffn_up, ffn_down, moe_router, moe_gate, moe_up, moe_down):
    batch, seq, d = x.shape
    depth = norm_mix.shape[0]
    assert d % LANES == 0 and seq % WKV_CHUNK == 0
    n = batch * seq
    h = x.astype(F32).reshape(n, d)
    v_first = None
    for i in range(depth):
        j = i // 2
        last = i == depth - 1
        if i % 2 == 0:
            v_mix = None if j == 0 else (rwkv_v0[j - 1], rwkv_v1[j - 1], rwkv_v2[j - 1])
            r, lw, k, v, kk, b, g = _rwkv_proj(
                h, seq, norm_mix[i], rwkv_mu[j], rwkv_w_rkv[j], rwkv_w0[j], rwkv_w1[j], rwkv_w2[j],
                rwkv_a0[j], rwkv_a1[j], rwkv_a2[j], rwkv_g1[j], rwkv_g2[j], rwkv_k_k[j], rwkv_k_a[j],
                v_first, v_mix)
            if v_first is None:
                v_first = v
            yg = _wkv(r, lw, k, v, kk, b, g, rwkv_r_k[j], rwkv_lnx_w[j], rwkv_lnx_b[j], seq)
            h = _wo_ffn(h, yg, rwkv_w_o[j], norm_ffn[i], ffn_gate[j], ffn_up[j], ffn_down[j])
            if last:
                h = _final_norm(h, final_norm)
        else:
            h = _s5_glu(h, seq, norm_mix[i], s5_lam_re[j], s5_lam_im[j], s5_log_step[j], s5_b_re[j], s5_b_im[j],
                        s5_c_re[j], s5_c_im[j], s5_d[j], s5_w_glu[j], s5_b_glu[j])
            h = _moe(h, norm_ffn[i], moe_router[j], j, moe_gate, moe_up, moe_down,
                     final_norm if last else None)
    return h.reshape(batch, seq, d).astype(x.dtype)


def _final_norm_kernel(h_ref, g_ref, o_ref):
    o_ref[...] = _rms(h_ref[...], g_ref[...])


def _final_norm(h, gain):
    n, d = h.shape
    rows = pl.BlockSpec((FFN_ROWS, d), lambda i: (i, 0))
    return pl.pallas_call(
        _final_norm_kernel, grid=(n // FFN_ROWS,), in_specs=[rows, _full((1, d))], out_specs=rows,
        out_shape=jax.ShapeDtypeStruct((n, d), F32), compiler_params=_params("arbitrary"),
        name="final_norm",
    )(h, gain.reshape(1, d).astype(F32))
```

```python
import functools
import math

import jax
import jax.numpy as jnp
from jax import lax
from jax.experimental import pallas as pl
from jax.experimental.pallas import tpu as pltpu

F32 = jnp.float32
BF16 = jnp.bfloat16

NORM_EPS = 1e-6
LNX_EPS = 64e-5
HEAD_DIM = 64
S5_GROUP = 16
S5_STATE = 64
S5_EIG_MAX = -1e-4
N_EXPERTS = 8
LANES = 128
SUBLANES = 8
WKV_CHUNK = 64
WKV_BATCH_ROWS = 2
VMEM_LIMIT = 56 * 1024 * 1024

PROJ_ROWS = 256
FFN_ROWS = 512
FFN_COLS = 1408
S5_ROWS = 512
ROUTER_ROWS = 512
MOE_ROWS = 768
MOE_COLS = 896
NEG_BIG = -1e30


def _params(*semantics):
    return pltpu.CompilerParams(dimension_semantics=semantics, vmem_limit_bytes=VMEM_LIMIT)


def _dot(a, b):
    return jnp.dot(a.astype(BF16), b.astype(BF16), preferred_element_type=F32)


def _dot_nt(a, b):
    return lax.dot_general(a.astype(BF16), b.astype(BF16), (((1,), (1,)), ((), ())),
                           preferred_element_type=F32)


def _dot_tn(a, b):
    return lax.dot_general(a.astype(BF16), b.astype(BF16), (((0,), (0,)), ((), ())),
                           preferred_element_type=F32)


def _split2(x):
    hi = x.astype(BF16)
    lo = (x - hi.astype(F32)).astype(BF16)
    return hi, lo


def _split3(x):
    hi = x.astype(BF16)
    r1 = x - hi.astype(F32)
    mid = r1.astype(BF16)
    lo = (r1 - mid.astype(F32)).astype(BF16)
    return hi, mid, lo


def _rms(x, g):
    return x * lax.rsqrt(jnp.mean(x * x, axis=-1, keepdims=True) + NORM_EPS) * g


def _sigmoid(x):
    return 1.0 / (1.0 + jnp.exp(-x))


def _full(shape):
    n = len(shape)
    return pl.BlockSpec(shape, lambda *_: (0,) * n)


def _rwkv_proj_kernel(*refs, has_vmix, tiles_per_seq):
    (x_ref, xp_ref, gn_ref, mu_ref, wrkv_ref, w0_ref, w1_ref, w2_ref, a0_ref, a1_ref, a2_ref,
     g1_ref, g2_ref, kk_ref, ka_ref, hs_ref, hst_ref) = refs[:17]
    rest = refs[17:]
    if has_vmix:
        vf_ref, v0_ref, v1_ref, v2_ref = rest[:4]
        rest = rest[4:]
    r_o, lw_o, k_o, v_o, kk_o, b_o, g_o = rest

    i = pl.program_id(0)
    gn = gn_ref[...]
    hn = _rms(x_ref[...], gn)
    starts_seq = (i % tiles_per_seq) == 0
    prev = _rms(xp_ref[SUBLANES - 1:SUBLANES, :], gn) * jnp.where(starts_seq, 0.0, 1.0)
    first_row = lax.broadcasted_iota(jnp.int32, hn.shape, 0) == 0
    shifted = jnp.where(first_row, prev, pltpu.roll(hn, 1, axis=0))
    xx = shifted - hn
    xr, xk, xv, xw, xa, xg = [(hn + xx * mu_ref[s:s + 1, :]).astype(BF16) for s in range(6)]

    r = jnp.dot(xr, wrkv_ref[0], preferred_element_type=F32)
    k = jnp.dot(xk, wrkv_ref[1], preferred_element_type=F32)
    v = jnp.dot(xv, wrkv_ref[2], preferred_element_type=F32)

    z = w0_ref[...] + _dot(jnp.tanh(_dot(xw, w1_ref[...])), w2_ref[...])
    lw_o[...] = -math.exp(-0.5) * _sigmoid(z)

    if has_vmix:
        mixv = _sigmoid(v0_ref[...] + _dot(_dot(xv, v1_ref[...]), v2_ref[...]))
        v = v + (vf_ref[...].astype(F32) - v) * mixv
    a = _sigmoid(a0_ref[...] + _dot(_dot(xa, a1_ref[...]), a2_ref[...]))
    g = _dot(_sigmoid(_dot(xg, g1_ref[...])), g2_ref[...])

    kk = k * kk_ref[...]
    hi, lo = _split2(kk * kk)
    ss = (jnp.dot(hi, hs_ref[...], preferred_element_type=F32)
          + jnp.dot(lo, hs_ref[...], preferred_element_type=F32))
    inv = 1.0 / jnp.maximum(jnp.sqrt(ss), 1e-12)
    hi, lo = _split2(inv)
    inv_full = (jnp.dot(hi, hst_ref[...], preferred_element_type=F32)
                + jnp.dot(lo, hst_ref[...], preferred_element_type=F32))
    kk = kk * inv_full
    k = k * (1.0 + (a - 1.0) * ka_ref[...])

    r_o[...] = r.astype(BF16)
    k_o[...] = k.astype(BF16)
    v_o[...] = v.astype(BF16)
    kk_o[...] = kk.astype(BF16)
    b_o[...] = (kk * a).astype(BF16)
    g_o[...] = g.astype(BF16)


def _pad_cols(w, n):
    return jnp.pad(w, ((0, 0), (0, n - w.shape[1])))


def _pad_rows(w, n):
    return jnp.pad(w, ((0, n - w.shape[0]), (0, 0)))


def _rwkv_proj(h, seq, gn, mu, w_rkv, w0, w1, w2, a0, a1, a2, g1, g2, k_k, k_a, v_first, v_mix):
    n, d = h.shape
    row2 = lambda p: p.reshape(1, d).astype(F32)
    lora = lambda down, up: (_pad_cols(down, _rup(down.shape[1])).astype(BF16),
                             _pad_rows(up, _rup(up.shape[0])).astype(BF16))
    w1p, w2p = lora(w1, w2)
    a1p, a2p = lora(a1, a2)
    g1p, g2p = lora(g1, g2)
    head_of = jnp.arange(d, dtype=jnp.int32) // HEAD_DIM
    hs = (head_of[:, None] == jnp.arange(LANES, dtype=jnp.int32)[None, :]).astype(BF16)
    hst = hs.T
    has_vmix = v_mix is not None
    tm = PROJ_ROWS
    assert seq % tm == 0
    row_spec = pl.BlockSpec((tm, d), lambda i: (i, 0))
    prev_spec = pl.BlockSpec((SUBLANES, d), lambda i: (jnp.maximum(i * (tm // SUBLANES) - 1, 0), 0))
    args = [h, h, row2(gn), mu.astype(F32), w_rkv.astype(BF16), row2(w0), w1p, w2p, row2(a0), a1p, a2p,
            g1p, g2p, row2(k_k), row2(k_a), hs, hst]
    specs = [row_spec, prev_spec] + [_full(a.shape) for a in args[2:]]
    if has_vmix:
        v0, v1, v2 = v_mix
        v1p, v2p = lora(v1, v2)
        extra = [v_first, row2(v0), v1p, v2p]
        args += extra
        specs += [row_spec] + [_full(a.shape) for a in extra[1:]]
    bf = jax.ShapeDtypeStruct((n, d), BF16)
    out_shape = [bf, jax.ShapeDtypeStruct((n, d), F32), bf, bf, bf, bf, bf]
    return pl.pallas_call(
        functools.partial(_rwkv_proj_kernel, has_vmix=has_vmix, tiles_per_seq=seq // tm),
        grid=(n // tm,),
        in_specs=specs,
        out_specs=[row_spec] * 7,
        out_shape=out_shape,
        compiler_params=_params("arbitrary"),
        name="rwkv_proj",
    )(*args)


def _rup(x, m=LANES):
    return (x + m - 1) // m * m


def _unit_lower_inverses(mats, row, col):
    same = lambda nb: (row // nb) == (col // nb)
    eye = jnp.where(row == col, 1.0, 0.0)
    n1 = [jnp.where(same(8), a, 0.0) for a in mats]
    n2 = [_dot(x, x) for x in n1]
    n4 = [_dot(x, x) for x in n2]
    t = [eye + x for x in n1]
    t = [ti + _dot(ti, x) for ti, x in zip(t, n2)]
    t = [ti + _dot(ti, x) for ti, x in zip(t, n4)]
    size = 8
    while size < WKV_CHUNK:
        grow = jnp.logical_and(same(2 * size), jnp.logical_not(same(size)))
        left = [_dot(ti, jnp.where(grow, a, 0.0)) for ti, a in zip(t, mats)]
        t = [ti + _dot(li, ti) for ti, li in zip(t, left)]
        size *= 2
    return t


def _wkv_kernel(r_ref, lw_ref, k_ref, v_ref, kk_ref, b_ref, g_ref, rk_ref, lnw_ref, lnb_ref,
                o_ref, s_ref):
    @pl.when(pl.program_id(1) == 0)
    def _():
        s_ref[...] = jnp.zeros_like(s_ref)

    nb, chunk, d = r_ref.shape
    pairs = d // LANES
    pr = 2 * chunk
    row = lax.broadcasted_iota(jnp.int32, (pr, pr), 0)
    col = lax.broadcasted_iota(jnp.int32, (pr, pr), 1)
    same_head = (row // chunk) == (col // chunk)
    incl = jnp.logical_and(same_head, row >= col)
    incl2 = jnp.concatenate([incl, incl], axis=1)
    strict = jnp.logical_and(same_head, row > col)
    crow = lax.broadcasted_iota(jnp.int32, (chunk, chunk), 0)
    ccol = lax.broadcasted_iota(jnp.int32, (chunk, chunk), 1)
    tri = jnp.where(crow >= ccol, 1.0, 0.0).astype(BF16)
    lane = lambda p: slice(p * LANES, (p + 1) * LANES)

    def block_diag(x, p):
        xp = x[:, lane(p)]
        return jnp.where(same_head, jnp.concatenate([xp, xp], axis=0), 0.0).astype(BF16)

    def stack_heads(x):
        return jnp.concatenate([x[:, :HEAD_DIM], x[:, HEAD_DIM:]], axis=0)

    ar, bk, bke, vs, p_end = [], [], [], [], []
    for bi in range(nb):
        lw = lw_ref[bi]
        lp = sum(jnp.dot(tri, part, preferred_element_type=F32) for part in _split3(lw))
        lp_end = lp[chunk - 1:chunk, :]
        r = r_ref[bi].astype(F32)
        k = k_ref[bi].astype(F32)
        kk = kk_ref[bi].astype(F32)
        b = b_ref[bi].astype(F32)
        p_inv = jnp.exp(-lp)
        to_end = jnp.exp(lp_end - lp)
        a_t = -kk * jnp.exp(lp - lw)
        b_t = b * p_inv
        k_t = k * p_inv
        r_t = r * jnp.exp(lp)
        b_e = b * to_end
        k_e = k * to_end
        v_b = v_ref[bi]
        for p in range(pairs):
            ar.append(jnp.concatenate([block_diag(a_t, p), block_diag(r_t, p)], axis=0))
            bk.append(jnp.concatenate([block_diag(b_t, p), block_diag(k_t, p)], axis=0))
            bke.append(jnp.concatenate([block_diag(b_e, p), block_diag(k_e, p)], axis=0))
            vs.append(stack_heads(v_b[:, lane(p)]))
            p_end.append(jnp.exp(lp_end[:, lane(p)]))

    qs = range(nb * pairs)
    s = [s_ref[q] for q in qs]
    gram = [_dot_nt(ar[q], bk[q]) for q in qs]
    ars = [_dot_nt(ar[q], s[q]) for q in qs]
    a_ab = [jnp.where(strict, gram[q][:pr, :pr], 0.0) for q in qs]
    a_ak = [jnp.where(strict, gram[q][:pr, pr:], 0.0) for q in qs]
    r_bk = [jnp.where(incl2, gram[q][pr:, :], 0.0).astype(BF16) for q in qs]
    w_in = [ars[q][:pr] + _dot(a_ak[q], vs[q]) for q in qs]
    t_inv = _unit_lower_inverses(a_ab, row, col)
    u = [_dot(t_inv[q], w_in[q]) for q in qs]
    uv = [jnp.concatenate([u[q].astype(BF16), vs[q]], axis=0) for q in qs]
    y = [ars[q][pr:] + jnp.dot(r_bk[q], uv[q], preferred_element_type=F32) for q in qs]
    for q in qs:
        s_ref[q] = s[q] * p_end[q] + _dot_tn(uv[q], bke[q])

    def head_norm(yh):
        mean = jnp.mean(yh, axis=-1, keepdims=True)
        yc = yh - mean
        var = jnp.mean(yc * yc, axis=-1, keepdims=True)
        return yc * lax.rsqrt(var + LNX_EPS)

    left = lax.broadcasted_iota(jnp.int32, (chunk, LANES), 1) < HEAD_DIM
    for bi in range(nb):
        rkk = r_ref[bi].astype(F32) * k_ref[bi].astype(F32) * rk_ref[...]
        yn, dots = [], []
        for p in range(pairs):
            yq = y[bi * pairs + p]
            yn.append(jnp.concatenate([head_norm(yq[:chunk]), head_norm(yq[chunk:])], axis=-1))
            rk_p = rkk[:, lane(p)]
            dots.append(jnp.where(left, jnp.sum(rk_p[:, :HEAD_DIM], axis=-1, keepdims=True),
                                  jnp.sum(rk_p[:, HEAD_DIM:], axis=-1, keepdims=True)))
        yn = jnp.concatenate(yn, axis=-1)
        bonus = jnp.concatenate(dots, axis=-1) * v_ref[bi].astype(F32)
        o_ref[bi] = ((yn * lnw_ref[...] + lnb_ref[...] + bonus) * g_ref[bi].astype(F32)).astype(BF16)


def _wkv(r, lw, k, v, kk, b, g, r_k, lnx_w, lnx_b, seq):
    n, d = r.shape
    batch = n // seq
    nb = WKV_BATCH_ROWS
    assert batch % nb == 0
    view = lambda a: a.reshape(batch, seq, d)
    blk = pl.BlockSpec((nb, WKV_CHUNK, d), lambda bi, c: (bi, c, 0))
    vec = pl.BlockSpec((1, d), lambda bi, c: (0, 0))
    row2 = lambda p: p.reshape(1, d).astype(F32)
    out = pl.pallas_call(
        _wkv_kernel,
        grid=(batch // nb, seq // WKV_CHUNK),
        in_specs=[blk] * 7 + [vec] * 3,
        out_specs=blk,
        out_shape=jax.ShapeDtypeStruct((batch, seq, d), BF16),
        scratch_shapes=[pltpu.VMEM((nb * (d // LANES), HEAD_DIM, LANES), F32)],
        compiler_params=_params("arbitrary", "arbitrary"),
        name="wkv7",
    )(view(r), view(lw), view(k), view(v), view(kk), view(b), view(g), row2(r_k), row2(lnx_w), row2(lnx_b))
    return out.reshape(n, d)


def _wo_ffn_kernel(h_ref, yg_ref, wo_ref, gn_ref, wg_ref, wu_ref, wd_ref, o_ref, h1_s, hn_s, acc_s):
    f = pl.program_id(1)

    @pl.when(f == 0)
    def _():
        h1 = h_ref[...] + jnp.dot(yg_ref[...], wo_ref[...], preferred_element_type=F32)
        h1_s[...] = h1
        hn_s[...] = _rms(h1, gn_ref[...]).astype(BF16)
        acc_s[...] = jnp.zeros_like(acc_s)

    x = hn_s[...]
    gate = jnp.dot(x, wg_ref[...], preferred_element_type=F32)
    up = jnp.dot(x, wu_ref[...], preferred_element_type=F32)
    act = (gate * _sigmoid(gate) * up).astype(BF16)
    acc_s[...] += jnp.dot(act, wd_ref[...], preferred_element_type=F32)

    @pl.when(f == pl.num_programs(1) - 1)
    def _():
        o_ref[...] = h1_s[...] + acc_s[...]


def _wo_ffn(h, yg, w_o, gn, w_gate, w_up, w_down):
    n, d = h.shape
    ff = w_gate.shape[1]
    tm, tf = FFN_ROWS, FFN_COLS
    rows = pl.BlockSpec((tm, d), lambda i, f: (i, 0))
    return pl.pallas_call(
        _wo_ffn_kernel,
        grid=(n // tm, ff // tf),
        in_specs=[rows, rows, _full((d, d)), _full((1, d)),
                  pl.BlockSpec((d, tf), lambda i, f: (0, f)),
                  pl.BlockSpec((d, tf), lambda i, f: (0, f)),
                  pl.BlockSpec((tf, d), lambda i, f: (f, 0))],
        out_specs=rows,
        out_shape=jax.ShapeDtypeStruct((n, d), F32),
        scratch_shapes=[pltpu.VMEM((tm, d), F32), pltpu.VMEM((tm, d), BF16), pltpu.VMEM((tm, d), F32)],
        compiler_params=_params("arbitrary", "arbitrary"),
        name="wo_ffn",
    )(h, yg, w_o.astype(BF16), gn.reshape(1, d).astype(F32),
      w_gate.astype(BF16), w_up.astype(BF16), w_down.astype(BF16))


def _s5_disc_kernel(lr_ref, li_ref, dt_ref, bre_ref, bim_ref, are_o, aim_o, bbre_o, bbim_o):
    lr = jnp.minimum(lr_ref[...], S5_EIG_MAX)
    li = li_ref[...]
    dt = dt_ref[...]
    dr, di = lr * dt, li * dt
    e = jnp.exp(dr)
    abar_re = e * jnp.cos(di)
    abar_im = e * jnp.sin(di)
    sh = jnp.sin(0.5 * di)
    num_re = (e - 1.0) * jnp.cos(di) - 2.0 * sh * sh
    den = lr * lr + li * li
    coef_re = (num_re * lr + abar_im * li) / den
    coef_im = (abar_im * lr - num_re * li) / den
    are_o[...] = abar_re
    aim_o[...] = abar_im
    bre, bim = bre_ref[...], bim_ref[...]
    bbre_o[...] = coef_re * bre - coef_im * bim
    bbim_o[...] = coef_re * bim + coef_im * bre


def _s5_discretise(lam_re, lam_im, log_step, b_re, b_im):
    groups, states = lam_re.shape
    q = b_re.shape[-1]
    gp = groups * states
    col = lambda a: a.reshape(gp, 1).astype(F32)
    dt = jnp.exp(log_step.astype(F32))
    dt_col = jnp.broadcast_to(dt[:, None], (groups, states)).reshape(gp, 1)
    one = jax.ShapeDtypeStruct((gp, 1), F32)
    wide = jax.ShapeDtypeStruct((gp, q), F32)
    return pl.pallas_call(
        _s5_disc_kernel,
        out_shape=[one, one, wide, wide],
        name="s5_discretise",
    )(col(lam_re), col(lam_im), dt_col, b_re.reshape(gp, q).astype(F32), b_im.reshape(gp, q).astype(F32))


def _s5_glu_kernel(h_ref, gn_ref, wb_ref, are_ref, aim_ref, wcre_ref, wcim_ref, dsk_ref, wglu_ref, bglu_ref,
                   o_ref, st_s, xt_s, bu_s, y_s):
    @pl.when(pl.program_id(0) == 0)
    def _():
        st_s[...] = jnp.zeros_like(st_s)

    batch, steps, d = h_ref.shape
    slabs = d // LANES
    half = st_s.shape[-1] // 2
    lane = lambda s: slice(s * LANES, (s + 1) * LANES)
    gn = gn_ref[...]
    dsk = dsk_ref[...]

    for b in range(batch):
        hn_b = _rms(h_ref[b], gn)
        for s in range(slabs):
            xt_s[s, pl.ds(b, steps, stride=batch), :] = hn_b[:, lane(s)]

    for s in range(slabs):
        sl = lane(s)
        xs = xt_s[s]
        bu_s[...] = jnp.dot(xs.astype(BF16), wb_ref[s], preferred_element_type=F32)
        a_re = are_ref[s]
        a_im = aim_ref[s]

        def step(t, carry):
            h_re, h_im = carry
            rows = pl.ds(pl.multiple_of(t * batch, batch), batch)
            n_re = a_re * h_re - a_im * h_im + bu_s[rows, :half]
            n_im = a_re * h_im + a_im * h_re + bu_s[rows, half:]
            bu_s[rows, :half] = n_re
            bu_s[rows, half:] = n_im
            return n_re, n_im

        h_re, h_im = lax.fori_loop(0, steps, step, (st_s[s, :, :half], st_s[s, :, half:]), unroll=8)
        st_s[s, :, :half] = h_re
        st_s[s, :, half:] = h_im
        y = (jnp.dot(bu_s[:, :half].astype(BF16), wcre_ref[s], preferred_element_type=F32)
             - jnp.dot(bu_s[:, half:].astype(BF16), wcim_ref[s], preferred_element_type=F32))
        y = y + dsk[:, sl] * xs
        y = 0.5 * y * (1.0 + jnp.tanh(math.sqrt(2.0 / math.pi) * (y + 0.044715 * (y * y * y))))
        y_s[:, sl] = y.astype(BF16)

    z = jnp.dot(y_s[...], wglu_ref[...], preferred_element_type=F32) + bglu_ref[...]
    glu = z[:, :d] * _sigmoid(z[:, d:])
    for s in range(slabs):
        xt_s[s] = glu[:, lane(s)]
    for b in range(batch):
        for s in range(slabs):
            o_ref[b, :, lane(s)] = h_ref[b, :, lane(s)] + xt_s[s, pl.ds(b, steps, stride=batch), :]


def _s5_glu(h, seq, gn, lam_re, lam_im, log_step, b_re, b_im, c_re, c_im, d_skip, w_glu, b_glu):
    n, d = h.shape
    groups, states = lam_re.shape
    q = S5_GROUP
    slabs = d // LANES
    gl = LANES // q
    half = gl * states
    abar_re, abar_im, bbar_re, bbar_im = _s5_discretise(lam_re, lam_im, log_step, b_re, b_im)
    eye = jnp.eye(gl, dtype=F32)

    def in_proj(bb):
        bb = bb.reshape(slabs, gl, states, q)
        return jnp.einsum('sgpc,gh->sgchp', bb, eye).reshape(slabs, LANES, half)

    def out_proj(c):
        c = c.astype(F32).reshape(slabs, gl, q, states)
        return jnp.einsum('sgcp,gh->sgphc', c, eye).reshape(slabs, half, LANES).astype(BF16)

    wb = jnp.concatenate([in_proj(bbar_re), in_proj(bbar_im)], axis=-1).astype(BF16)
    lanes_of = lambda a: jnp.broadcast_to(a.reshape(slabs, 1, half), (slabs, SUBLANES, half))
    batch = n // seq
    assert batch == SUBLANES
    steps = S5_ROWS // batch
    tm = S5_ROWS
    rows = pl.BlockSpec((batch, steps, d), lambda i: (0, i, 0))
    args = [h.reshape(batch, seq, d), gn.reshape(1, d).astype(F32), wb, lanes_of(abar_re), lanes_of(abar_im),
            out_proj(c_re), out_proj(c_im), d_skip.reshape(1, d).astype(F32),
            w_glu.astype(BF16), b_glu.reshape(1, 2 * d).astype(F32)]
    out = pl.pallas_call(
        _s5_glu_kernel,
        grid=(seq // steps,),
        in_specs=[rows] + [_full(a.shape) for a in args[1:]],
        out_specs=rows,
        out_shape=jax.ShapeDtypeStruct((batch, seq, d), F32),
        scratch_shapes=[pltpu.VMEM((slabs, batch, 2 * half), F32),
                        pltpu.VMEM((slabs, tm, LANES), F32),
                        pltpu.VMEM((tm, 2 * half), F32),
                        pltpu.VMEM((tm, d), BF16)],
        compiler_params=_params("arbitrary"),
        name="s5_glu",
    )(*args)
    return out.reshape(n, d)


def _router_kernel(h_ref, gn_ref, rhi_ref, rlo_ref, xn_o, meta_o, gate_o, cnt_o, base_s):
    @pl.when(pl.program_id(0) == 0)
    def _():
        base_s[...] = jnp.zeros_like(base_s)

    tm = h_ref.shape[0]
    hn = _rms(h_ref[...], gn_ref[...])
    xn_o[...] = hn
    hi, lo = _split2(hn)
    logits = (jnp.dot(hi, rhi_ref[...], preferred_element_type=F32)
              + jnp.dot(hi, rlo_ref[...], preferred_element_type=F32)
              + jnp.dot(lo, rhi_ref[...], preferred_element_type=F32))
    lane = lax.broadcasted_iota(jnp.int32, (tm, LANES), 1).astype(F32)
    logits = jnp.where(lane < N_EXPERTS, logits, NEG_BIG)
    m1 = jnp.max(logits, axis=-1, keepdims=True)
    i1 = jnp.min(jnp.where(logits == m1, lane, float(LANES)), axis=-1, keepdims=True)
    rest = jnp.where(lane == i1, NEG_BIG, logits)
    m2 = jnp.max(rest, axis=-1, keepdims=True)
    i2 = jnp.min(jnp.where(rest == m2, lane, float(LANES)), axis=-1, keepdims=True)
    e = jnp.exp(m2 - m1)
    g1 = 1.0 / (1.0 + e)
    g2 = e / (1.0 + e)

    oh1 = jnp.where(lane == i1, 1.0, 0.0)
    oh2 = jnp.where(lane == i2, 1.0, 0.0)
    both = oh1 + oh2
    row = lax.broadcasted_iota(jnp.int32, (tm, tm), 0)
    col = lax.broadcasted_iota(jnp.int32, (tm, tm), 1)
    before = jnp.where(row > col, 1.0, 0.0).astype(BF16)
    seen = jnp.dot(before, both.astype(BF16), preferred_element_type=F32) + base_s[...]
    rank1 = jnp.sum(oh1 * seen, axis=-1, keepdims=True)
    rank2 = jnp.sum(oh2 * seen, axis=-1, keepdims=True)
    base = base_s[...] + jnp.sum(both, axis=0, keepdims=True)
    base_s[...] = base
    cnt_o[...] = jnp.broadcast_to(base, cnt_o.shape)

    meta = jnp.where(lane == 0.0, i1, jnp.where(lane == 1.0, i2, jnp.where(lane == 2.0, rank1, rank2)))
    meta_o[...] = meta.astype(jnp.int32)
    gate_o[...] = jnp.where(lane == 0.0, g1, g2)


def _router(h, gn, router):
    n, d = h.shape
    tm = ROUTER_ROWS
    rp = _pad_cols(router.astype(F32), LANES)
    rhi = rp.astype(BF16)
    rlo = (rp - rhi.astype(F32)).astype(BF16)
    rows = pl.BlockSpec((tm, d), lambda i: (i, 0))
    narrow = pl.BlockSpec((tm, LANES), lambda i: (i, 0))
    return pl.pallas_call(
        _router_kernel,
        grid=(n // tm,),
        in_specs=[rows, _full((1, d)), _full((d, LANES)), _full((d, LANES))],
        out_specs=[rows, narrow, narrow, _full((SUBLANES, LANES))],
        out_shape=[jax.ShapeDtypeStruct((n, d), F32),
                   jax.ShapeDtypeStruct((n, LANES), jnp.int32),
                   jax.ShapeDtypeStruct((n, LANES), F32),
                   jax.ShapeDtypeStruct((SUBLANES, LANES), F32)],
        scratch_shapes=[pltpu.VMEM((1, LANES), F32)],
        compiler_params=_params("arbitrary"),
        name="moe_router",
    )(h, gn.reshape(1, d).astype(F32), rhi, rlo)


def _row_copy(src_ref, src_row, dst_ref, dst_row, sem):
    return pltpu.make_async_copy(src_ref.at[pl.ds(src_row, 1)], dst_ref.at[pl.ds(dst_row, 1)], sem)


def _slot_maps_kernel(slot_ref, seg_ref, src_o, dst_o, *, n, tm):
    total = src_o.shape[0]
    n_tiles = total // tm - 1

    def fill(first, last, shift):
        def body(s, carry):
            src_o[s] = 0
            dst_o[tm + s] = s + shift
            return carry
        lax.fori_loop(first, last, body, 0)

    def spare_shift(tile):
        return 2 * n + (tile % 3) * tm - tile * tm

    for e in range(N_EXPERTS):
        fill(seg_ref[e], seg_ref[N_EXPERTS + e], seg_ref[2 * N_EXPERTS + e])

    def idle_tile(t, carry):
        fill(t * tm, (t + 1) * tm, spare_shift(t))
        return carry
    lax.fori_loop(seg_ref[3 * N_EXPERTS], n_tiles, idle_tile, 0)

    def lead(s, carry):
        dst_o[s] = s + spare_shift(2) + 2 * tm
        src_o[total - tm + s] = 0
        return carry
    lax.fori_loop(0, tm, lead, 0, unroll=8)

    def place(a, carry):
        s = slot_ref[a]
        token = lax.shift_right_logical(a, 1)
        src_o[s] = token
        dst_o[tm + s] = (a & 1) * n + token
        return carry
    lax.fori_loop(0, 2 * n, place, 0, unroll=8)


def _slot_maps(slot, pad_start, cnt, pad_end, n, n_tiles):
    tm = MOE_ROWS
    total = (n_tiles + 1) * tm
    last_tile = (pad_end - 1) // tm
    shift = 2 * n + (last_tile % 3) * tm - last_tile * tm
    segments = jnp.concatenate([pad_start + cnt, pad_end, shift, pad_end[-1:] // tm]).astype(jnp.int32)
    smem = pl.BlockSpec(memory_space=pltpu.SMEM)
    return pl.pallas_call(
        functools.partial(_slot_maps_kernel, n=n, tm=tm),
        in_specs=[smem, smem],
        out_specs=[smem, smem],
        out_shape=[jax.ShapeDtypeStruct((total,), jnp.int32)] * 2,
        compiler_params=pltpu.CompilerParams(disable_bounds_checks=True),
        name="moe_slot_maps",
    )(slot, segments)


def _experts_kernel(te_ref, nu_ref, src_ref, dst_ref, xn_ref, wg_ref, wu_ref, wd_ref, out_ref,
                    xg_s, xb_s, acc_s, ob_s, gsem, osem, *, n_cols):
    del te_ref
    i = pl.program_id(0)
    f = pl.program_id(1)
    n_used = nu_ref[0]
    used = i < n_used
    tm = acc_s.shape[0]
    per_step = tm // n_cols
    slot = i % 2
    other = 1 - slot

    def gather_row(tile, buf, r):
        _row_copy(xn_ref, src_ref[tile * tm + r], xg_s.at[buf], r, gsem.at[buf]).start(priority=1)

    def scatter_row(tile, buf, r):
        _row_copy(ob_s.at[buf], r, out_ref, dst_ref[(tile + 1) * tm + r], osem.at[buf]).start(priority=1)

    def wait_gather(buf):
        pltpu.make_async_copy(xn_ref.at[pl.ds(0, tm)], xg_s.at[buf], gsem.at[buf]).wait()

    def wait_scatter(buf):
        pltpu.make_async_copy(ob_s.at[buf], out_ref.at[pl.ds(0, tm)], osem.at[buf]).wait()

    @pl.when(jnp.logical_and(used, f == 0))
    def _():
        @pl.when(i == 0)
        def _():
            def body(r, carry):
                gather_row(0, 0, r)
                return carry
            lax.fori_loop(0, tm, body, 0, unroll=8)
            ob_s[...] = jnp.zeros_like(ob_s)
            spare0 = out_ref.shape[0] - 3 * tm
            for k in range(3):
                pltpu.make_async_copy(ob_s.at[0], out_ref.at[pl.ds(spare0 + k * tm, tm)], osem.at[0]).start()
            for k in range(3):
                wait_scatter(0)

        wait_gather(slot)
        xb_s[...] = xg_s[slot].astype(BF16)
        acc_s[...] = jnp.zeros_like(acc_s)

    @pl.when(used)
    def _():
        for u in range(per_step):
            r = f * per_step + u
            gather_row(i + 1, other, r)
            scatter_row(i - 1, other, r)
        x = xb_s[...]
        gate = jnp.dot(x, wg_ref[0, 0].astype(BF16), preferred_element_type=F32)
        up = jnp.dot(x, wu_ref[0, 0].astype(BF16), preferred_element_type=F32)
        act = (gate * _sigmoid(gate) * up).astype(BF16)
        acc_s[...] += jnp.dot(act, wd_ref[0, 0].astype(BF16), preferred_element_type=F32)

    @pl.when(jnp.logical_and(used, f == n_cols - 1))
    def _():
        @pl.when(i > 0)
        def _():
            wait_scatter(slot)

        ob_s[slot] = acc_s[...]

        @pl.when(i == n_used - 1)
        def _():
            def body(r, carry):
                scatter_row(i, slot, r)
                return carry
            lax.fori_loop(0, tm, body, 0, unroll=8)
            wait_scatter(other)
            wait_scatter(slot)
            wait_gather(other)


def _experts(xn, tile_expert, n_used, src, dst, layer, w_gate, w_up, w_down):
    n, d = xn.shape
    fe = w_gate.shape[-1]
    tm, tf = MOE_ROWS, MOE_COLS
    nf = fe // tf
    assert tm % nf == 0
    n_tiles = tile_expert.shape[0]
    col = lambda i, f, nu: jnp.where(i < nu[0], f, nf - 1)
    return pl.pallas_call(
        functools.partial(_experts_kernel, n_cols=nf),
        grid_spec=pltpu.PrefetchScalarGridSpec(
            num_scalar_prefetch=4,
            grid=(n_tiles, nf),
            in_specs=[pl.BlockSpec(memory_space=pl.ANY),
                      pl.BlockSpec((1, 1, d, tf), lambda i, f, te, nu, s, t: (layer, te[i], 0, col(i, f, nu))),
                      pl.BlockSpec((1, 1, d, tf), lambda i, f, te, nu, s, t: (layer, te[i], 0, col(i, f, nu))),
                      pl.BlockSpec((1, 1, tf, d), lambda i, f, te, nu, s, t: (layer, te[i], col(i, f, nu), 0))],
            out_specs=pl.BlockSpec(memory_space=pl.ANY),
            scratch_shapes=[pltpu.VMEM((2, tm, d), F32), pltpu.VMEM((tm, d), BF16),
                            pltpu.VMEM((tm, d), F32), pltpu.VMEM((2, tm, d), F32),
                            pltpu.SemaphoreType.DMA((2,)), pltpu.SemaphoreType.DMA((2,))],
        ),
        out_shape=jax.ShapeDtypeStruct((2 * n + 3 * tm, d), F32),
        compiler_params=_params("arbitrary", "arbitrary"),
        name="moe_experts",
    )(tile_expert, n_used, src, dst, xn, w_gate, w_up, w_down)


def _combine_kernel(h_ref, gate_ref, fn_ref, e0_ref, e1_ref, o_ref, *, final_norm):
    gates = gate_ref[...]
    out = h_ref[...] + gates[:, 0:1] * e0_ref[...] + gates[:, 1:2] * e1_ref[...]
    if final_norm:
        out = _rms(out, fn_ref[...])
    o_ref[...] = out


def _combine(h, gates, eo, final_gain):
    n, d = h.shape
    tm = FFN_ROWS
    final_norm = final_gain is not None
    fn = (final_gain if final_norm else jnp.ones((d,), F32)).reshape(1, d).astype(F32)
    rows = pl.BlockSpec((tm, d), lambda i: (i, 0))
    return pl.pallas_call(
        functools.partial(_combine_kernel, final_norm=final_norm),
        grid=(n // tm,),
        in_specs=[rows, pl.BlockSpec((tm, LANES), lambda i: (i, 0)), _full((1, d)),
                  rows, pl.BlockSpec((tm, d), lambda i: (i + n // tm, 0))],
        out_specs=rows,
        out_shape=jax.ShapeDtypeStruct((n, d), F32),
        compiler_params=_params("arbitrary"),
        name="moe_combine",
    )(h, gates, fn, eo, eo)


def _moe(h, gn, router, layer, w_gate, w_up, w_down, final_gain):
    n, d = h.shape
    tm = MOE_ROWS
    xn, meta, gates, counts = _router(h, gn, router)
    n_tiles = -(-2 * n // tm) + N_EXPERTS
    cnt = counts[0, :N_EXPERTS].astype(jnp.int32)
    padded = (cnt + tm - 1) // tm * tm
    pad_end = jnp.cumsum(padded)
    pad_start = pad_end - padded
    slot = (pad_start[meta[:, 0:2]] + meta[:, 2:4]).reshape(-1).astype(jnp.int32)
    tile_start = jnp.arange(n_tiles, dtype=jnp.int32) * tm
    tile_expert = jnp.minimum(jnp.sum(tile_start[:, None] >= pad_end[None, :], axis=1),
                              N_EXPERTS - 1).astype(jnp.int32)
    n_used = (pad_end[-1:] // tm).astype(jnp.int32)
    src, dst = _slot_maps(slot, pad_start, cnt, pad_end, n, n_tiles)
    eo = _experts(xn, tile_expert, n_used, src, dst, layer, w_gate, w_up, w_down)
    return _combine(h, gates, eo, final_gain)


def kernel(x, norm_mix, norm_ffn, final_norm, rwkv_mu, rwkv_w_rkv, rwkv_w0, rwkv_w1, rwkv_w2, rwkv_a0, rwkv_a1, rwkv_a2, rwkv_g1, rwkv_g2, rwkv_k_k, rwkv_k_a, rwkv_r_k, rwkv_lnx_w, rwkv_lnx_b, rwkv_w_o, rwkv_v0, rwkv_v1, rwkv_v2, s5_lam_re, s5_lam_im, s5_log_step, s5_b_re, s5_b_im, s5_c_re, s5_c_im, s5_d, s5_w_glu, s5_b_glu, ffn_gate, ffn_up, ffn_down, moe_router, moe_gate, moe_up, moe_down):
    batch, seq, d = x.shape
    depth = norm_mix.shape[0]
    assert d % LANES == 0 and seq % WKV_CHUNK == 0
    n = batch * seq
    h = x.astype(F32).reshape(n, d)
    v_first = None
    for i in range(depth):
        j = i // 2
        last = i == depth - 1
        if i % 2 == 0:
            v_mix = None if j == 0 else (rwkv_v0[j - 1], rwkv_v1[j - 1], rwkv_v2[j - 1])
            r, lw, k, v, kk, b, g = _rwkv_proj(
                h, seq, norm_mix[i], rwkv_mu[j], rwkv_w_rkv[j], rwkv_w0[j], rwkv_w1[j], rwkv_w2[j],
                rwkv_a0[j], rwkv_a1[j], rwkv_a2[j], rwkv_g1[j], rwkv_g2[j], rwkv_k_k[j], rwkv_k_a[j],
                v_first, v_mix)
            if v_first is None:
                v_first = v
            yg = _wkv(r, lw, k, v, kk, b, g, rwkv_r_k[j], rwkv_lnx_w[j], rwkv_lnx_b[j], seq)
            h = _wo_ffn(h, yg, rwkv_w_o[j], norm_ffn[i], ffn_gate[j], ffn_up[j], ffn_down[j])
            if last:
                h = _final_norm(h, final_norm)
        else:
            h = _s5_glu(h, seq, norm_mix[i], s5_lam_re[j], s5_lam_im[j], s5_log_step[j], s5_b_re[j], s5_b_im[j],
                        s5_c_re[j], s5_c_im[j], s5_d[j], s5_w_glu[j], s5_b_glu[j])
            h = _moe(h, norm_ffn[i], moe_router[j], j, moe_gate, moe_up, moe_down,
                     final_norm if last else None)
    return h.reshape(batch, seq, d).astype(x.dtype)


def _final_norm_kernel(h_ref, g_ref, o_ref):
    o_ref[...] = _rms(h_ref[...], g_ref[...])


def _final_norm(h, gain):
    n, d = h.shape
    rows = pl.BlockSpec((FFN_ROWS, d), lambda i: (i, 0))
    return pl.pallas_call(
        _final_norm_kernel, grid=(n // FFN_ROWS,), in_specs=[rows, _full((1, d))], out_specs=rows,
        out_shape=jax.ShapeDtypeStruct((n, d), F32), compiler_params=_params("arbitrary"),
        name="final_norm",
    )(h, gain.reshape(1, d).astype(F32))
```

```python
import functools
import math

import jax
import jax.numpy as jnp
from jax import lax
from jax.experimental import pallas as pl
from jax.experimental.pallas import tpu as pltpu

F32 = jnp.float32
BF16 = jnp.bfloat16

NORM_EPS = 1e-6
LNX_EPS = 64e-5
HEAD_DIM = 64
S5_GROUP = 16
S5_STATE = 64
S5_EIG_MAX = -1e-4
N_EXPERTS = 8
LANES = 128
SUBLANES = 8
WKV_CHUNK = 64
WKV_BATCH_ROWS = 4
VMEM_LIMIT = 56 * 1024 * 1024

PROJ_ROWS = 256
FFN_ROWS = 512
FFN_COLS = 1408
S5_ROWS = 512
ROUTER_ROWS = 512
MOE_ROWS = 768
MOE_COLS = 896
NEG_BIG = -1e30


def _params(*semantics):
    return pltpu.CompilerParams(dimension_semantics=semantics, vmem_limit_bytes=VMEM_LIMIT)


def _dot(a, b):
    return jnp.dot(a.astype(BF16), b.astype(BF16), preferred_element_type=F32)


def _dot_nt(a, b):
    return lax.dot_general(a.astype(BF16), b.astype(BF16), (((1,), (1,)), ((), ())),
                           preferred_element_type=F32)


def _dot_tn(a, b):
    return lax.dot_general(a.astype(BF16), b.astype(BF16), (((0,), (0,)), ((), ())),
                           preferred_element_type=F32)


def _split2(x):
    hi = x.astype(BF16)
    lo = (x - hi.astype(F32)).astype(BF16)
    return hi, lo


def _split3(x):
    hi = x.astype(BF16)
    r1 = x - hi.astype(F32)
    mid = r1.astype(BF16)
    lo = (r1 - mid.astype(F32)).astype(BF16)
    return hi, mid, lo


def _rms(x, g):
    return x * lax.rsqrt(jnp.mean(x * x, axis=-1, keepdims=True) + NORM_EPS) * g


def _sigmoid(x):
    return 1.0 / (1.0 + jnp.exp(-x))


def _full(shape):
    n = len(shape)
    return pl.BlockSpec(shape, lambda *_: (0,) * n)


def _rwkv_proj_kernel(*refs, has_vmix, tiles_per_seq):
    (x_ref, xp_ref, gn_ref, mu_ref, wrkv_ref, w0_ref, w1_ref, w2_ref, a0_ref, a1_ref, a2_ref,
     g1_ref, g2_ref, kk_ref, ka_ref, hs_ref, hst_ref) = refs[:17]
    rest = refs[17:]
    if has_vmix:
        vf_ref, v0_ref, v1_ref, v2_ref = rest[:4]
        rest = rest[4:]
    r_o, lw_o, k_o, v_o, kk_o, b_o, g_o = rest

    i = pl.program_id(0)
    gn = gn_ref[...]
    hn = _rms(x_ref[...], gn)
    starts_seq = (i % tiles_per_seq) == 0
    prev = _rms(xp_ref[SUBLANES - 1:SUBLANES, :], gn) * jnp.where(starts_seq, 0.0, 1.0)
    first_row = lax.broadcasted_iota(jnp.int32, hn.shape, 0) == 0
    shifted = jnp.where(first_row, prev, pltpu.roll(hn, 1, axis=0))
    xx = shifted - hn
    xr, xk, xv, xw, xa, xg = [(hn + xx * mu_ref[s:s + 1, :]).astype(BF16) for s in range(6)]

    r = jnp.dot(xr, wrkv_ref[0], preferred_element_type=F32)
    k = jnp.dot(xk, wrkv_ref[1], preferred_element_type=F32)
    v = jnp.dot(xv, wrkv_ref[2], preferred_element_type=F32)

    z = w0_ref[...] + _dot(jnp.tanh(_dot(xw, w1_ref[...])), w2_ref[...])
    lw_o[...] = -math.exp(-0.5) * _sigmoid(z)

    if has_vmix:
        mixv = _sigmoid(v0_ref[...] + _dot(_dot(xv, v1_ref[...]), v2_ref[...]))
        v = v + (vf_ref[...].astype(F32) - v) * mixv
    a = _sigmoid(a0_ref[...] + _dot(_dot(xa, a1_ref[...]), a2_ref[...]))
    g = _dot(_sigmoid(_dot(xg, g1_ref[...])), g2_ref[...])

    kk = k * kk_ref[...]
    hi, lo = _split2(kk * kk)
    ss = (jnp.dot(hi, hs_ref[...], preferred_element_type=F32)
          + jnp.dot(lo, hs_ref[...], preferred_element_type=F32))
    inv = 1.0 / jnp.maximum(jnp.sqrt(ss), 1e-12)
    hi, lo = _split2(inv)
    inv_full = (jnp.dot(hi, hst_ref[...], preferred_element_type=F32)
                + jnp.dot(lo, hst_ref[...], preferred_element_type=F32))
    kk = kk * inv_full
    k = k * (1.0 + (a - 1.0) * ka_ref[...])

    r_o[...] = r.astype(BF16)
    k_o[...] = k.astype(BF16)
    v_o[...] = v.astype(BF16)
    kk_o[...] = kk.astype(BF16)
    b_o[...] = (kk * a).astype(BF16)
    g_o[...] = g.astype(BF16)


def _pad_cols(w, n):
    return jnp.pad(w, ((0, 0), (0, n - w.shape[1])))


def _pad_rows(w, n):
    return jnp.pad(w, ((0, n - w.shape[0]), (0, 0)))


def _rwkv_proj(h, seq, gn, mu, w_rkv, w0, w1, w2, a0, a1, a2, g1, g2, k_k, k_a, v_first, v_mix):
    n, d = h.shape
    row2 = lambda p: p.reshape(1, d).astype(F32)
    lora = lambda down, up: (_pad_cols(down, _rup(down.shape[1])).astype(BF16),
                             _pad_rows(up, _rup(up.shape[0])).astype(BF16))
    w1p, w2p = lora(w1, w2)
    a1p, a2p = lora(a1, a2)
    g1p, g2p = lora(g1, g2)
    head_of = jnp.arange(d, dtype=jnp.int32) // HEAD_DIM
    hs = (head_of[:, None] == jnp.arange(LANES, dtype=jnp.int32)[None, :]).astype(BF16)
    hst = hs.T
    has_vmix = v_mix is not None
    tm = PROJ_ROWS
    assert seq % tm == 0
    row_spec = pl.BlockSpec((tm, d), lambda i: (i, 0))
    prev_spec = pl.BlockSpec((SUBLANES, d), lambda i: (jnp.maximum(i * (tm // SUBLANES) - 1, 0), 0))
    args = [h, h, row2(gn), mu.astype(F32), w_rkv.astype(BF16), row2(w0), w1p, w2p, row2(a0), a1p, a2p,
            g1p, g2p, row2(k_k), row2(k_a), hs, hst]
    specs = [row_spec, prev_spec] + [_full(a.shape) for a in args[2:]]
    if has_vmix:
        v0, v1, v2 = v_mix
        v1p, v2p = lora(v1, v2)
        extra = [v_first, row2(v0), v1p, v2p]
        args += extra
        specs += [row_spec] + [_full(a.shape) for a in extra[1:]]
    bf = jax.ShapeDtypeStruct((n, d), BF16)
    out_shape = [bf, jax.ShapeDtypeStruct((n, d), F32), bf, bf, bf, bf, bf]
    return pl.pallas_call(
        functools.partial(_rwkv_proj_kernel, has_vmix=has_vmix, tiles_per_seq=seq // tm),
        grid=(n // tm,),
        in_specs=specs,
        out_specs=[row_spec] * 7,
        out_shape=out_shape,
        compiler_params=_params("arbitrary"),
        name="rwkv_proj",
    )(*args)


def _rup(x, m=LANES):
    return (x + m - 1) // m * m


def _unit_lower_inverses(mats, row, col):
    same = lambda nb: (row // nb) == (col // nb)
    eye = jnp.where(row == col, 1.0, 0.0)
    n1 = [jnp.where(same(8), a, 0.0) for a in mats]
    n2 = [_dot(x, x) for x in n1]
    n4 = [_dot(x, x) for x in n2]
    t = [eye + x for x in n1]
    t = [ti + _dot(ti, x) for ti, x in zip(t, n2)]
    t = [ti + _dot(ti, x) for ti, x in zip(t, n4)]
    size = 8
    while size < WKV_CHUNK:
        grow = jnp.logical_and(same(2 * size), jnp.logical_not(same(size)))
        left = [_dot(ti, jnp.where(grow, a, 0.0)) for ti, a in zip(t, mats)]
        t = [ti + _dot(li, ti) for ti, li in zip(t, left)]
        size *= 2
    return t


def _wkv_kernel(r_ref, lw_ref, k_ref, v_ref, kk_ref, b_ref, g_ref, rk_ref, lnw_ref, lnb_ref,
                o_ref, s_ref):
    @pl.when(pl.program_id(1) == 0)
    def _():
        s_ref[...] = jnp.zeros_like(s_ref)

    nb, chunk, d = r_ref.shape
    pairs = d // LANES
    pr = 2 * chunk
    row = lax.broadcasted_iota(jnp.int32, (pr, pr), 0)
    col = lax.broadcasted_iota(jnp.int32, (pr, pr), 1)
    same_head = (row // chunk) == (col // chunk)
    incl = jnp.logical_and(same_head, row >= col)
    incl2 = jnp.concatenate([incl, incl], axis=1)
    strict = jnp.logical_and(same_head, row > col)
    crow = lax.broadcasted_iota(jnp.int32, (chunk, chunk), 0)
    ccol = lax.broadcasted_iota(jnp.int32, (chunk, chunk), 1)
    tri = jnp.where(crow >= ccol, 1.0, 0.0).astype(BF16)
    lane = lambda p: slice(p * LANES, (p + 1) * LANES)

    def block_diag(x, p):
        xp = x[:, lane(p)]
        return jnp.where(same_head, jnp.concatenate([xp, xp], axis=0), 0.0).astype(BF16)

    def stack_heads(x):
        return jnp.concatenate([x[:, :HEAD_DIM], x[:, HEAD_DIM:]], axis=0)

    ar, bk, bke, vs, p_end = [], [], [], [], []
    for bi in range(nb):
        lw = lw_ref[bi]
        lp = sum(jnp.dot(tri, part, preferred_element_type=F32) for part in _split3(lw))
        lp_end = lp[chunk - 1:chunk, :]
        r = r_ref[bi].astype(F32)
        k = k_ref[bi].astype(F32)
        kk = kk_ref[bi].astype(F32)
        b = b_ref[bi].astype(F32)
        p_inv = jnp.exp(-lp)
        to_end = jnp.exp(lp_end - lp)
        a_t = -kk * jnp.exp(lp - lw)
        b_t = b * p_inv
        k_t = k * p_inv
        r_t = r * jnp.exp(lp)
        b_e = b * to_end
        k_e = k * to_end
        v_b = v_ref[bi]
        for p in range(pairs):
            ar.append(jnp.concatenate([block_diag(a_t, p), block_diag(r_t, p)], axis=0))
            bk.append(jnp.concatenate([block_diag(b_t, p), block_diag(k_t, p)], axis=0))
            bke.append(jnp.concatenate([block_diag(b_e, p), block_diag(k_e, p)], axis=0))
            vs.append(stack_heads(v_b[:, lane(p)]))
            p_end.append(jnp.exp(lp_end[:, lane(p)]))

    qs = range(nb * pairs)
    s = [s_ref[q] for q in qs]
    gram = [_dot_nt(ar[q], bk[q]) for q in qs]
    ars = [_dot_nt(ar[q], s[q]) for q in qs]
    a_ab = [jnp.where(strict, gram[q][:pr, :pr], 0.0) for q in qs]
    a_ak = [jnp.where(strict, gram[q][:pr, pr:], 0.0) for q in qs]
    r_bk = [jnp.where(incl2, gram[q][pr:, :], 0.0).astype(BF16) for q in qs]
    w_in = [ars[q][:pr] + _dot(a_ak[q], vs[q]) for q in qs]
    t_inv = _unit_lower_inverses(a_ab, row, col)
    u = [_dot(t_inv[q], w_in[q]) for q in qs]
    uv = [jnp.concatenate([u[q].astype(BF16), vs[q]], axis=0) for q in qs]
    y = [ars[q][pr:] + jnp.dot(r_bk[q], uv[q], preferred_element_type=F32) for q in qs]
    for q in qs:
        s_ref[q] = s[q] * p_end[q] + _dot_tn(uv[q], bke[q])

    def head_norm(yh):
        mean = jnp.mean(yh, axis=-1, keepdims=True)
        yc = yh - mean
        var = jnp.mean(yc * yc, axis=-1, keepdims=True)
        return yc * lax.rsqrt(var + LNX_EPS)

    left = lax.broadcasted_iota(jnp.int32, (chunk, LANES), 1) < HEAD_DIM
    for bi in range(nb):
        rkk = r_ref[bi].astype(F32) * k_ref[bi].astype(F32) * rk_ref[...]
        yn, dots = [], []
        for p in range(pairs):
            yq = y[bi * pairs + p]
            yn.append(jnp.concatenate([head_norm(yq[:chunk]), head_norm(yq[chunk:])], axis=-1))
            rk_p = rkk[:, lane(p)]
            dots.append(jnp.where(left, jnp.sum(rk_p[:, :HEAD_DIM], axis=-1, keepdims=True),
                                  jnp.sum(rk_p[:, HEAD_DIM:], axis=-1, keepdims=True)))
        yn = jnp.concatenate(yn, axis=-1)
        bonus = jnp.concatenate(dots, axis=-1) * v_ref[bi].astype(F32)
        o_ref[bi] = ((yn * lnw_ref[...] + lnb_ref[...] + bonus) * g_ref[bi].astype(F32)).astype(BF16)


def _wkv(r, lw, k, v, kk, b, g, r_k, lnx_w, lnx_b, seq):
    n, d = r.shape
    batch = n // seq
    nb = WKV_BATCH_ROWS
    assert batch % nb == 0
    view = lambda a: a.reshape(batch, seq, d)
    blk = pl.BlockSpec((nb, WKV_CHUNK, d), lambda bi, c: (bi, c, 0))
    vec = pl.BlockSpec((1, d), lambda bi, c: (0, 0))
    row2 = lambda p: p.reshape(1, d).astype(F32)
    out = pl.pallas_call(
        _wkv_kernel,
        grid=(batch // nb, seq // WKV_CHUNK),
        in_specs=[blk] * 7 + [vec] * 3,
        out_specs=blk,
        out_shape=jax.ShapeDtypeStruct((batch, seq, d), BF16),
        scratch_shapes=[pltpu.VMEM((nb * (d // LANES), HEAD_DIM, LANES), F32)],
        compiler_params=_params("arbitrary", "arbitrary"),
        name="wkv7",
    )(view(r), view(lw), view(k), view(v), view(kk), view(b), view(g), row2(r_k), row2(lnx_w), row2(lnx_b))
    return out.reshape(n, d)


def _wo_ffn_kernel(h_ref, yg_ref, wo_ref, gn_ref, wg_ref, wu_ref, wd_ref, o_ref, h1_s, hn_s, acc_s):
    f = pl.program_id(1)

    @pl.when(f == 0)
    def _():
        h1 = h_ref[...] + jnp.dot(yg_ref[...], wo_ref[...], preferred_element_type=F32)
        h1_s[...] = h1
        hn_s[...] = _rms(h1, gn_ref[...]).astype(BF16)
        acc_s[...] = jnp.zeros_like(acc_s)

    x = hn_s[...]
    gate = jnp.dot(x, wg_ref[...], preferred_element_type=F32)
    up = jnp.dot(x, wu_ref[...], preferred_element_type=F32)
    act = (gate * _sigmoid(gate) * up).astype(BF16)
    acc_s[...] += jnp.dot(act, wd_ref[...], preferred_element_type=F32)

    @pl.when(f == pl.num_programs(1) - 1)
    def _():
        o_ref[...] = h1_s[...] + acc_s[...]


def _wo_ffn(h, yg, w_o, gn, w_gate, w_up, w_down):
    n, d = h.shape
    ff = w_gate.shape[1]
    tm, tf = FFN_ROWS, FFN_COLS
    rows = pl.BlockSpec((tm, d), lambda i, f: (i, 0))
    return pl.pallas_call(
        _wo_ffn_kernel,
        grid=(n // tm, ff // tf),
        in_specs=[rows, rows, _full((d, d)), _full((1, d)),
                  pl.BlockSpec((d, tf), lambda i, f: (0, f)),
                  pl.BlockSpec((d, tf), lambda i, f: (0, f)),
                  pl.BlockSpec((tf, d), lambda i, f: (f, 0))],
        out_specs=rows,
        out_shape=jax.ShapeDtypeStruct((n, d), F32),
        scratch_shapes=[pltpu.VMEM((tm, d), F32), pltpu.VMEM((tm, d), BF16), pltpu.VMEM((tm, d), F32)],
        compiler_params=_params("arbitrary", "arbitrary"),
        name="wo_ffn",
    )(h, yg, w_o.astype(BF16), gn.reshape(1, d).astype(F32),
      w_gate.astype(BF16), w_up.astype(BF16), w_down.astype(BF16))


def _s5_disc_kernel(lr_ref, li_ref, dt_ref, bre_ref, bim_ref, are_o, aim_o, bbre_o, bbim_o):
    lr = jnp.minimum(lr_ref[...], S5_EIG_MAX)
    li = li_ref[...]
    dt = dt_ref[...]
    dr, di = lr * dt, li * dt
    e = jnp.exp(dr)
    abar_re = e * jnp.cos(di)
    abar_im = e * jnp.sin(di)
    sh = jnp.sin(0.5 * di)
    num_re = (e - 1.0) * jnp.cos(di) - 2.0 * sh * sh
    den = lr * lr + li * li
    coef_re = (num_re * lr + abar_im * li) / den
    coef_im = (abar_im * lr - num_re * li) / den
    are_o[...] = abar_re
    aim_o[...] = abar_im
    bre, bim = bre_ref[...], bim_ref[...]
    bbre_o[...] = coef_re * bre - coef_im * bim
    bbim_o[...] = coef_re * bim + coef_im * bre


def _s5_discretise(lam_re, lam_im, log_step, b_re, b_im):
    groups, states = lam_re.shape
    q = b_re.shape[-1]
    gp = groups * states
    col = lambda a: a.reshape(gp, 1).astype(F32)
    dt = jnp.exp(log_step.astype(F32))
    dt_col = jnp.broadcast_to(dt[:, None], (groups, states)).reshape(gp, 1)
    one = jax.ShapeDtypeStruct((gp, 1), F32)
    wide = jax.ShapeDtypeStruct((gp, q), F32)
    return pl.pallas_call(
        _s5_disc_kernel,
        out_shape=[one, one, wide, wide],
        name="s5_discretise",
    )(col(lam_re), col(lam_im), dt_col, b_re.reshape(gp, q).astype(F32), b_im.reshape(gp, q).astype(F32))


def _s5_glu_kernel(h_ref, gn_ref, wb_ref, are_ref, aim_ref, wcre_ref, wcim_ref, dsk_ref, wglu_ref, bglu_ref,
                   o_ref, st_s, xt_s, bu_s, y_s):
    @pl.when(pl.program_id(0) == 0)
    def _():
        st_s[...] = jnp.zeros_like(st_s)

    batch, steps, d = h_ref.shape
    slabs = d // LANES
    half = st_s.shape[-1] // 2
    lane = lambda s: slice(s * LANES, (s + 1) * LANES)
    gn = gn_ref[...]
    dsk = dsk_ref[...]

    for b in range(batch):
        hn_b = _rms(h_ref[b], gn)
        for s in range(slabs):
            xt_s[s, pl.ds(b, steps, stride=batch), :] = hn_b[:, lane(s)]

    def in_proj(s):
        bu_s[s % 2] = jnp.dot(xt_s[s].astype(BF16), wb_ref[s], preferred_element_type=F32)

    in_proj(0)
    for s in range(slabs):
        if s + 1 < slabs:
            in_proj(s + 1)
        sl = lane(s)
        buf = s % 2
        a_re = are_ref[s]
        a_im = aim_ref[s]
        h_re, h_im = st_s[s, :, :half], st_s[s, :, half:]
        for t in range(steps):
            rows = slice(t * batch, (t + 1) * batch)
            h_re, h_im = (a_re * h_re - a_im * h_im + bu_s[buf, rows, :half],
                          a_re * h_im + a_im * h_re + bu_s[buf, rows, half:])
            bu_s[buf, rows, :half] = h_re
            bu_s[buf, rows, half:] = h_im
        st_s[s, :, :half] = h_re
        st_s[s, :, half:] = h_im
        y = (jnp.dot(bu_s[buf, :, :half].astype(BF16), wcre_ref[s], preferred_element_type=F32)
             - jnp.dot(bu_s[buf, :, half:].astype(BF16), wcim_ref[s], preferred_element_type=F32))
        y = y + dsk[:, sl] * xt_s[s]
        y = 0.5 * y * (1.0 + jnp.tanh(math.sqrt(2.0 / math.pi) * (y + 0.044715 * (y * y * y))))
        y_s[:, sl] = y.astype(BF16)

    z = jnp.dot(y_s[...], wglu_ref[...], preferred_element_type=F32) + bglu_ref[...]
    glu = z[:, :d] * _sigmoid(z[:, d:])
    for s in range(slabs):
        xt_s[s] = glu[:, lane(s)]
    for b in range(batch):
        for s in range(slabs):
            o_ref[b, :, lane(s)] = h_ref[b, :, lane(s)] + xt_s[s, pl.ds(b, steps, stride=batch), :]


def _s5_glu(h, seq, gn, lam_re, lam_im, log_step, b_re, b_im, c_re, c_im, d_skip, w_glu, b_glu):
    n, d = h.shape
    groups, states = lam_re.shape
    q = S5_GROUP
    slabs = d // LANES
    gl = LANES // q
    half = gl * states
    abar_re, abar_im, bbar_re, bbar_im = _s5_discretise(lam_re, lam_im, log_step, b_re, b_im)
    eye = jnp.eye(gl, dtype=F32)

    def in_proj(bb):
        bb = bb.reshape(slabs, gl, states, q)
        return jnp.einsum('sgpc,gh->sgchp', bb, eye).reshape(slabs, LANES, half)

    def out_proj(c):
        c = c.astype(F32).reshape(slabs, gl, q, states)
        return jnp.einsum('sgcp,gh->sgphc', c, eye).reshape(slabs, half, LANES).astype(BF16)

    wb = jnp.concatenate([in_proj(bbar_re), in_proj(bbar_im)], axis=-1).astype(BF16)
    lanes_of = lambda a: jnp.broadcast_to(a.reshape(slabs, 1, half), (slabs, SUBLANES, half))
    batch = n // seq
    assert batch == SUBLANES
    steps = S5_ROWS // batch
    tm = S5_ROWS
    rows = pl.BlockSpec((batch, steps, d), lambda i: (0, i, 0))
    args = [h.reshape(batch, seq, d), gn.reshape(1, d).astype(F32), wb, lanes_of(abar_re), lanes_of(abar_im),
            out_proj(c_re), out_proj(c_im), d_skip.reshape(1, d).astype(F32),
            w_glu.astype(BF16), b_glu.reshape(1, 2 * d).astype(F32)]
    out = pl.pallas_call(
        _s5_glu_kernel,
        grid=(seq // steps,),
        in_specs=[rows] + [_full(a.shape) for a in args[1:]],
        out_specs=rows,
        out_shape=jax.ShapeDtypeStruct((batch, seq, d), F32),
        scratch_shapes=[pltpu.VMEM((slabs, batch, 2 * half), F32),
                        pltpu.VMEM((slabs, tm, LANES), F32),
                        pltpu.VMEM((2, tm, 2 * half), F32),
                        pltpu.VMEM((tm, d), BF16)],
        compiler_params=_params("arbitrary"),
        name="s5_glu",
    )(*args)
    return out.reshape(n, d)


def _router_kernel(h_ref, gn_ref, rhi_ref, rlo_ref, xn_o, meta_o, gate_o, cnt_o, base_s):
    @pl.when(pl.program_id(0) == 0)
    def _():
        base_s[...] = jnp.zeros_like(base_s)

    tm = h_ref.shape[0]
    hn = _rms(h_ref[...], gn_ref[...])
    xn_o[...] = hn
    hi, lo = _split2(hn)
    logits = (jnp.dot(hi, rhi_ref[...], preferred_element_type=F32)
              + jnp.dot(hi, rlo_ref[...], preferred_element_type=F32)
              + jnp.dot(lo, rhi_ref[...], preferred_element_type=F32))
    lane = lax.broadcasted_iota(jnp.int32, (tm, LANES), 1).astype(F32)
    logits = jnp.where(lane < N_EXPERTS, logits, NEG_BIG)
    m1 = jnp.max(logits, axis=-1, keepdims=True)
    i1 = jnp.min(jnp.where(logits == m1, lane, float(LANES)), axis=-1, keepdims=True)
    rest = jnp.where(lane == i1, NEG_BIG, logits)
    m2 = jnp.max(rest, axis=-1, keepdims=True)
    i2 = jnp.min(jnp.where(rest == m2, lane, float(LANES)), axis=-1, keepdims=True)
    e = jnp.exp(m2 - m1)
    g1 = 1.0 / (1.0 + e)
    g2 = e / (1.0 + e)

    oh1 = jnp.where(lane == i1, 1.0, 0.0)
    oh2 = jnp.where(lane == i2, 1.0, 0.0)
    both = oh1 + oh2
    row = lax.broadcasted_iota(jnp.int32, (tm, tm), 0)
    col = lax.broadcasted_iota(jnp.int32, (tm, tm), 1)
    before = jnp.where(row > col, 1.0, 0.0).astype(BF16)
    seen = jnp.dot(before, both.astype(BF16), preferred_element_type=F32) + base_s[...]
    rank1 = jnp.sum(oh1 * seen, axis=-1, keepdims=True)
    rank2 = jnp.sum(oh2 * seen, axis=-1, keepdims=True)
    base = base_s[...] + jnp.sum(both, axis=0, keepdims=True)
    base_s[...] = base
    cnt_o[...] = jnp.broadcast_to(base, cnt_o.shape)

    meta = jnp.where(lane == 0.0, i1, jnp.where(lane == 1.0, i2, jnp.where(lane == 2.0, rank1, rank2)))
    meta_o[...] = meta.astype(jnp.int32)
    gate_o[...] = jnp.where(lane == 0.0, g1, g2)


def _router(h, gn, router):
    n, d = h.shape
    tm = ROUTER_ROWS
    rp = _pad_cols(router.astype(F32), LANES)
    rhi = rp.astype(BF16)
    rlo = (rp - rhi.astype(F32)).astype(BF16)
    rows = pl.BlockSpec((tm, d), lambda i: (i, 0))
    narrow = pl.BlockSpec((tm, LANES), lambda i: (i, 0))
    return pl.pallas_call(
        _router_kernel,
        grid=(n // tm,),
        in_specs=[rows, _full((1, d)), _full((d, LANES)), _full((d, LANES))],
        out_specs=[rows, narrow, narrow, _full((SUBLANES, LANES))],
        out_shape=[jax.ShapeDtypeStruct((n, d), F32),
                   jax.ShapeDtypeStruct((n, LANES), jnp.int32),
                   jax.ShapeDtypeStruct((n, LANES), F32),
                   jax.ShapeDtypeStruct((SUBLANES, LANES), F32)],
        scratch_shapes=[pltpu.VMEM((1, LANES), F32)],
        compiler_params=_params("arbitrary"),
        name="moe_router",
    )(h, gn.reshape(1, d).astype(F32), rhi, rlo)


def _row_copy(src_ref, src_row, dst_ref, dst_row, sem):
    return pltpu.make_async_copy(src_ref.at[pl.ds(src_row, 1)], dst_ref.at[pl.ds(dst_row, 1)], sem)


def _slot_maps_kernel(slot_ref, seg_ref, src_o, dst_o, *, n, tm):
    total = src_o.shape[0]
    n_tiles = total // tm - 1

    def fill(first, last, shift):
        def body(s, carry):
            src_o[s] = 0
            dst_o[tm + s] = s + shift
            return carry
        lax.fori_loop(first, last, body, 0)

    def spare_shift(tile):
        return 2 * n + (tile % 3) * tm - tile * tm

    for e in range(N_EXPERTS):
        fill(seg_ref[e], seg_ref[N_EXPERTS + e], seg_ref[2 * N_EXPERTS + e])

    def idle_tile(t, carry):
        fill(t * tm, (t + 1) * tm, spare_shift(t))
        return carry
    lax.fori_loop(seg_ref[3 * N_EXPERTS], n_tiles, idle_tile, 0)

    def lead(s, carry):
        dst_o[s] = s + spare_shift(2) + 2 * tm
        src_o[total - tm + s] = 0
        return carry
    lax.fori_loop(0, tm, lead, 0, unroll=8)

    def place(a, carry):
        s = slot_ref[a]
        token = lax.shift_right_logical(a, 1)
        src_o[s] = token
        dst_o[tm + s] = (a & 1) * n + token
        return carry
    lax.fori_loop(0, 2 * n, place, 0, unroll=8)


def _slot_maps(slot, pad_start, cnt, pad_end, n, n_tiles):
    tm = MOE_ROWS
    total = (n_tiles + 1) * tm
    last_tile = (pad_end - 1) // tm
    shift = 2 * n + (last_tile % 3) * tm - last_tile * tm
    segments = jnp.concatenate([pad_start + cnt, pad_end, shift, pad_end[-1:] // tm]).astype(jnp.int32)
    smem = pl.BlockSpec(memory_space=pltpu.SMEM)
    return pl.pallas_call(
        functools.partial(_slot_maps_kernel, n=n, tm=tm),
        in_specs=[smem, smem],
        out_specs=[smem, smem],
        out_shape=[jax.ShapeDtypeStruct((total,), jnp.int32)] * 2,
        compiler_params=pltpu.CompilerParams(disable_bounds_checks=True),
        name="moe_slot_maps",
    )(slot, segments)


def _experts_kernel(te_ref, nu_ref, src_ref, dst_ref, xn_ref, wg_ref, wu_ref, wd_ref, out_ref,
                    xg_s, xb_s, acc_s, ob_s, gsem, osem, *, n_cols):
    del te_ref
    i = pl.program_id(0)
    f = pl.program_id(1)
    n_used = nu_ref[0]
    used = i < n_used
    tm = acc_s.shape[0]
    per_step = tm // n_cols
    slot = i % 2
    other = 1 - slot

    def gather_row(tile, buf, r):
        _row_copy(xn_ref, src_ref[tile * tm + r], xg_s.at[buf], r, gsem.at[buf]).start(priority=1)

    def scatter_row(tile, buf, r):
        _row_copy(ob_s.at[buf], r, out_ref, dst_ref[(tile + 1) * tm + r], osem.at[buf]).start(priority=0)

    def wait_gather(buf):
        pltpu.make_async_copy(xn_ref.at[pl.ds(0, tm)], xg_s.at[buf], gsem.at[buf]).wait()

    def wait_scatter(buf):
        pltpu.make_async_copy(ob_s.at[buf], out_ref.at[pl.ds(0, tm)], osem.at[buf]).wait()

    @pl.when(jnp.logical_and(used, f == 0))
    def _():
        @pl.when(i == 0)
        def _():
            def body(r, carry):
                gather_row(0, 0, r)
                return carry
            lax.fori_loop(0, tm, body, 0, unroll=8)
            ob_s[...] = jnp.zeros_like(ob_s)
            spare0 = out_ref.shape[0] - 3 * tm
            for k in range(3):
                pltpu.make_async_copy(ob_s.at[0], out_ref.at[pl.ds(spare0 + k * tm, tm)], osem.at[0]).start()
            for k in range(3):
                wait_scatter(0)

        wait_gather(slot)
        xb_s[...] = xg_s[slot].astype(BF16)
        acc_s[...] = jnp.zeros_like(acc_s)

    @pl.when(used)
    def _():
        for u in range(per_step):
            r = f * per_step + u
            gather_row(i + 1, other, r)
            scatter_row(i - 1, other, r)
        x = xb_s[...]
        gate = jnp.dot(x, wg_ref[0, 0].astype(BF16), preferred_element_type=F32)
        up = jnp.dot(x, wu_ref[0, 0].astype(BF16), preferred_element_type=F32)
        act = (gate * _sigmoid(gate) * up).astype(BF16)
        acc_s[...] += jnp.dot(act, wd_ref[0, 0].astype(BF16), preferred_element_type=F32)

    @pl.when(jnp.logical_and(used, f == n_cols - 1))
    def _():
        @pl.when(i > 0)
        def _():
            wait_scatter(slot)

        ob_s[slot] = acc_s[...]

        @pl.when(i == n_used - 1)
        def _():
            def body(r, carry):
                scatter_row(i, slot, r)
                return carry
            lax.fori_loop(0, tm, body, 0, unroll=8)
            wait_scatter(other)
            wait_scatter(slot)
            wait_gather(other)


def _experts(xn, tile_expert, n_used, src, dst, layer, w_gate, w_up, w_down):
    n, d = xn.shape
    fe = w_gate.shape[-1]
    tm, tf = MOE_ROWS, MOE_COLS
    nf = fe // tf
    assert tm % nf == 0
    n_tiles = tile_expert.shape[0]
    col = lambda i, f, nu: jnp.where(i < nu[0], f, nf - 1)
    return pl.pallas_call(
        functools.partial(_experts_kernel, n_cols=nf),
        grid_spec=pltpu.PrefetchScalarGridSpec(
            num_scalar_prefetch=4,
            grid=(n_tiles, nf),
            in_specs=[pl.BlockSpec(memory_space=pl.ANY),
                      pl.BlockSpec((1, 1, d, tf), lambda i, f, te, nu, s, t: (layer, te[i], 0, col(i, f, nu))),
                      pl.BlockSpec((1, 1, d, tf), lambda i, f, te, nu, s, t: (layer, te[i], 0, col(i, f, nu))),
                      pl.BlockSpec((1, 1, tf, d), lambda i, f, te, nu, s, t: (layer, te[i], col(i, f, nu), 0))],
            out_specs=pl.BlockSpec(memory_space=pl.ANY),
            scratch_shapes=[pltpu.VMEM((2, tm, d), F32), pltpu.VMEM((tm, d), BF16),
                            pltpu.VMEM((tm, d), F32), pltpu.VMEM((2, tm, d), F32),
                            pltpu.SemaphoreType.DMA((2,)), pltpu.SemaphoreType.DMA((2,))],
        ),
        out_shape=jax.ShapeDtypeStruct((2 * n + 3 * tm, d), F32),
        compiler_params=_params("arbitrary", "arbitrary"),
        name="moe_experts",
    )(tile_expert, n_used, src, dst, xn, w_gate, w_up, w_down)


def _combine_kernel(h_ref, gate_ref, fn_ref, e0_ref, e1_ref, o_ref, *, final_norm):
    gates = gate_ref[...]
    out = h_ref[...] + gates[:, 0:1] * e0_ref[...] + gates[:, 1:2] * e1_ref[...]
    if final_norm:
        out = _rms(out, fn_ref[...])
    o_ref[...] = out


def _combine(h, gates, eo, final_gain):
    n, d = h.shape
    tm = FFN_ROWS
    final_norm = final_gain is not None
    fn = (final_gain if final_norm else jnp.ones((d,), F32)).reshape(1, d).astype(F32)
    rows = pl.BlockSpec((tm, d), lambda i: (i, 0))
    return pl.pallas_call(
        functools.partial(_combine_kernel, final_norm=final_norm),
        grid=(n // tm,),
        in_specs=[rows, pl.BlockSpec((tm, LANES), lambda i: (i, 0)), _full((1, d)),
                  rows, pl.BlockSpec((tm, d), lambda i: (i + n // tm, 0))],
        out_specs=rows,
        out_shape=jax.ShapeDtypeStruct((n, d), F32),
        compiler_params=_params("arbitrary"),
        name="moe_combine",
    )(h, gates, fn, eo, eo)


def _moe(h, gn, router, layer, w_gate, w_up, w_down, final_gain):
    n, d = h.shape
    tm = MOE_ROWS
    xn, meta, gates, counts = _router(h, gn, router)
    n_tiles = -(-2 * n // tm) + N_EXPERTS
    cnt = counts[0, :N_EXPERTS].astype(jnp.int32)
    padded = (cnt + tm - 1) // tm * tm
    pad_end = jnp.cumsum(padded)
    pad_start = pad_end - padded
    slot = (pad_start[meta[:, 0:2]] + meta[:, 2:4]).reshape(-1).astype(jnp.int32)
    tile_start = jnp.arange(n_tiles, dtype=jnp.int32) * tm
    tile_expert = jnp.minimum(jnp.sum(tile_start[:, None] >= pad_end[None, :], axis=1),
                              N_EXPERTS - 1).astype(jnp.int32)
    n_used = (pad_end[-1:] // tm).astype(jnp.int32)
    src, dst = _slot_maps(slot, pad_start, cnt, pad_end, n, n_tiles)
    eo = _experts(xn, tile_expert, n_used, src, dst, layer, w_gate, w_up, w_down)
    return _combine(h, gates, eo, final_gain)


def kernel(x, norm_mix, norm_ffn, final_norm, rwkv_mu, rwkv_w_rkv, rwkv_w0, rwkv_w1, rwkv_w2, rwkv_a0, rwkv_a1, rwkv_a2, rwkv_g1, rwkv_g2, rwkv_k_k, rwkv_k_a, rwkv_r_k, rwkv_lnx_w, rwkv_lnx_b, rwkv_w_o, rwkv_v0, rwkv_v1, rwkv_v2, s5_lam_re, s5_lam_im, s5_log_step, s5_b_re, s5_b_im, s5_c_re, s5_c_im, s5_d, s5_w_glu, s5_b_glu, ffn_gate, ffn_up, ffn_down, moe_router, moe_gate, moe_up, moe_down):
    batch, seq, d = x.shape
    depth = norm_mix.shape[0]
    assert d % LANES == 0 and seq % WKV_CHUNK == 0
    n = batch * seq
    h = x.astype(F32).reshape(n, d)
    v_first = None
    for i in range(depth):
        j = i // 2
        last = i == depth - 1
        if i % 2 == 0:
            v_mix = None if j == 0 else (rwkv_v0[j - 1], rwkv_v1[j - 1], rwkv_v2[j - 1])
            r, lw, k, v, kk, b, g = _rwkv_proj(
                h, seq, norm_mix[i], rwkv_mu[j], rwkv_w_rkv[j], rwkv_w0[j], rwkv_w1[j], rwkv_w2[j],
                rwkv_a0[j], rwkv_a1[j], rwkv_a2[j], rwkv_g1[j], rwkv_g2[j], rwkv_k_k[j], rwkv_k_a[j],
                v_first, v_mix)
            if v_first is None:
                v_first = v
            yg = _wkv(r, lw, k, v, kk, b, g, rwkv_r_k[j], rwkv_lnx_w[j], rwkv_lnx_b[j], seq)
            h = _wo_ffn(h, yg, rwkv_w_o[j], norm_ffn[i], ffn_gate[j], ffn_up[j], ffn_down[j])
            if last:
                h = _final_norm(h, final_norm)
        else:
            h = _s5_glu(h, seq, norm_mix[i], s5_lam_re[j], s5_lam_im[j], s5_log_step[j], s5_b_re[j], s5_b_im[j],
                        s5_c_re[j], s5_c_im[j], s5_d[j], s5_w_glu[j], s5_b_glu[j])
            h = _moe(h, norm_ffn[i], moe_router[j], j, moe_gate, moe_up, moe_down,
                     final_norm if last else None)
    return h.reshape(batch, seq, d).astype(x.dtype)


def _final_norm_kernel(h_ref, g_ref, o_ref):
    o_ref[...] = _rms(h_ref[...], g_ref[...])


def _final_norm(h, gain):
    n, d = h.shape
    rows = pl.BlockSpec((FFN_ROWS, d), lambda i: (i, 0))
    return pl.pallas_call(
        _final_norm_kernel, grid=(n // FFN_ROWS,), in_specs=[rows, _full((1, d))], out_specs=rows,
        out_shape=jax.ShapeDtypeStruct((n, d), F32), compiler_params=_params("arbitrary"),
        name="final_norm",
    )(h, gain.reshape(1, d).astype(F32))
```

```python
import functools
import math

import jax
import jax.numpy as jnp
from jax import lax
from jax.experimental import pallas as pl
from jax.experimental.pallas import tpu as pltpu

F32 = jnp.float32
BF16 = jnp.bfloat16

NORM_EPS = 1e-6
LNX_EPS = 64e-5
HEAD_DIM = 64
S5_GROUP = 16
S5_STATE = 64
S5_EIG_MAX = -1e-4
N_EXPERTS = 8
LANES = 128
SUBLANES = 8
WKV_CHUNK = 64
WKV_BATCH_ROWS = 4
VMEM_LIMIT = 56 * 1024 * 1024

PROJ_ROWS = 512
FFN_ROWS = 512
FFN_COLS = 1408
S5_ROWS = 512
ROUTER_ROWS = 512
MOE_ROWS = 768
MOE_COLS = 896
NEG_BIG = -1e30


def _params(*semantics):
    return pltpu.CompilerParams(dimension_semantics=semantics, vmem_limit_bytes=VMEM_LIMIT)


def _dot(a, b):
    return jnp.dot(a.astype(BF16), b.astype(BF16), preferred_element_type=F32)


def _dot_nt(a, b):
    return lax.dot_general(a.astype(BF16), b.astype(BF16), (((1,), (1,)), ((), ())),
                           preferred_element_type=F32)


def _dot_tn(a, b):
    return lax.dot_general(a.astype(BF16), b.astype(BF16), (((0,), (0,)), ((), ())),
                           preferred_element_type=F32)


def _split2(x):
    hi = x.astype(BF16)
    lo = (x - hi.astype(F32)).astype(BF16)
    return hi, lo


def _split3(x):
    hi = x.astype(BF16)
    r1 = x - hi.astype(F32)
    mid = r1.astype(BF16)
    lo = (r1 - mid.astype(F32)).astype(BF16)
    return hi, mid, lo


def _rms(x, g):
    return x * lax.rsqrt(jnp.mean(x * x, axis=-1, keepdims=True) + NORM_EPS) * g


def _sigmoid(x):
    return 1.0 / (1.0 + jnp.exp(-x))


def _full(shape):
    n = len(shape)
    return pl.BlockSpec(shape, lambda *_: (0,) * n)


def _rwkv_proj_kernel(*refs, has_vmix, tiles_per_seq):
    (x_ref, xp_ref, gn_ref, mu_ref, wrkv_ref, w0_ref, w1_ref, w2_ref, a0_ref, a1_ref, a2_ref,
     g1_ref, g2_ref, kk_ref, ka_ref, hs_ref, hst_ref) = refs[:17]
    rest = refs[17:]
    if has_vmix:
        vf_ref, v0_ref, v1_ref, v2_ref = rest[:4]
        rest = rest[4:]
    r_o, lw_o, k_o, v_o, kk_o, b_o, g_o = rest

    i = pl.program_id(0)
    gn = gn_ref[...]
    hn = _rms(x_ref[...], gn)
    starts_seq = (i % tiles_per_seq) == 0
    prev = _rms(xp_ref[SUBLANES - 1:SUBLANES, :], gn) * jnp.where(starts_seq, 0.0, 1.0)
    first_row = lax.broadcasted_iota(jnp.int32, hn.shape, 0) == 0
    shifted = jnp.where(first_row, prev, pltpu.roll(hn, 1, axis=0))
    xx = shifted - hn
    xr, xk, xv, xw, xa, xg = [(hn + xx * mu_ref[s:s + 1, :]).astype(BF16) for s in range(6)]

    r = jnp.dot(xr, wrkv_ref[0], preferred_element_type=F32)
    k = jnp.dot(xk, wrkv_ref[1], preferred_element_type=F32)
    v = jnp.dot(xv, wrkv_ref[2], preferred_element_type=F32)

    z = w0_ref[...] + _dot(jnp.tanh(_dot(xw, w1_ref[...])), w2_ref[...])
    lw_o[...] = -math.exp(-0.5) * _sigmoid(z)

    if has_vmix:
        mixv = _sigmoid(v0_ref[...] + _dot(_dot(xv, v1_ref[...]), v2_ref[...]))
        v = v + (vf_ref[...].astype(F32) - v) * mixv
    a = _sigmoid(a0_ref[...] + _dot(_dot(xa, a1_ref[...]), a2_ref[...]))
    g = _dot(_sigmoid(_dot(xg, g1_ref[...])), g2_ref[...])

    kk = k * kk_ref[...]
    hi, lo = _split2(kk * kk)
    ss = (jnp.dot(hi, hs_ref[...], preferred_element_type=F32)
          + jnp.dot(lo, hs_ref[...], preferred_element_type=F32))
    inv = 1.0 / jnp.maximum(jnp.sqrt(ss), 1e-12)
    hi, lo = _split2(inv)
    inv_full = (jnp.dot(hi, hst_ref[...], preferred_element_type=F32)
                + jnp.dot(lo, hst_ref[...], preferred_element_type=F32))
    kk = kk * inv_full
    k = k * (1.0 + (a - 1.0) * ka_ref[...])

    r_o[...] = r.astype(BF16)
    k_o[...] = k.astype(BF16)
    v_o[...] = v.astype(BF16)
    kk_o[...] = kk.astype(BF16)
    b_o[...] = (kk * a).astype(BF16)
    g_o[...] = g.astype(BF16)


def _pad_cols(w, n):
    return jnp.pad(w, ((0, 0), (0, n - w.shape[1])))


def _pad_rows(w, n):
    return jnp.pad(w, ((0, n - w.shape[0]), (0, 0)))


def _rwkv_proj(h, seq, gn, mu, w_rkv, w0, w1, w2, a0, a1, a2, g1, g2, k_k, k_a, v_first, v_mix):
    n, d = h.shape
    row2 = lambda p: p.reshape(1, d).astype(F32)
    lora = lambda down, up: (_pad_cols(down, _rup(down.shape[1])).astype(BF16),
                             _pad_rows(up, _rup(up.shape[0])).astype(BF16))
    w1p, w2p = lora(w1, w2)
    a1p, a2p = lora(a1, a2)
    g1p, g2p = lora(g1, g2)
    head_of = jnp.arange(d, dtype=jnp.int32) // HEAD_DIM
    hs = (head_of[:, None] == jnp.arange(LANES, dtype=jnp.int32)[None, :]).astype(BF16)
    hst = hs.T
    has_vmix = v_mix is not None
    tm = PROJ_ROWS
    assert seq % tm == 0
    row_spec = pl.BlockSpec((tm, d), lambda i: (i, 0))
    prev_spec = pl.BlockSpec((SUBLANES, d), lambda i: (jnp.maximum(i * (tm // SUBLANES) - 1, 0), 0))
    args = [h, h, row2(gn), mu.astype(F32), w_rkv.astype(BF16), row2(w0), w1p, w2p, row2(a0), a1p, a2p,
            g1p, g2p, row2(k_k), row2(k_a), hs, hst]
    specs = [row_spec, prev_spec] + [_full(a.shape) for a in args[2:]]
    if has_vmix:
        v0, v1, v2 = v_mix
        v1p, v2p = lora(v1, v2)
        extra = [v_first, row2(v0), v1p, v2p]
        args += extra
        specs += [row_spec] + [_full(a.shape) for a in extra[1:]]
    bf = jax.ShapeDtypeStruct((n, d), BF16)
    out_shape = [bf, jax.ShapeDtypeStruct((n, d), F32), bf, bf, bf, bf, bf]
    return pl.pallas_call(
        functools.partial(_rwkv_proj_kernel, has_vmix=has_vmix, tiles_per_seq=seq // tm),
        grid=(n // tm,),
        in_specs=specs,
        out_specs=[row_spec] * 7,
        out_shape=out_shape,
        compiler_params=_params("arbitrary"),
        name="rwkv_proj",
    )(*args)


def _rup(x, m=LANES):
    return (x + m - 1) // m * m


def _unit_lower_inverses(mats, row, col):
    same = lambda nb: (row // nb) == (col // nb)
    eye = jnp.where(row == col, 1.0, 0.0)
    n1 = [jnp.where(same(8), a, 0.0) for a in mats]
    n2 = [_dot(x, x) for x in n1]
    n4 = [_dot(x, x) for x in n2]
    t = [eye + x for x in n1]
    t = [ti + _dot(ti, x) for ti, x in zip(t, n2)]
    t = [ti + _dot(ti, x) for ti, x in zip(t, n4)]
    size = 8
    while size < WKV_CHUNK:
        grow = jnp.logical_and(same(2 * size), jnp.logical_not(same(size)))
        left = [_dot(ti, jnp.where(grow, a, 0.0)) for ti, a in zip(t, mats)]
        t = [ti + _dot(li, ti) for ti, li in zip(t, left)]
        size *= 2
    return t


def _wkv_kernel(r_ref, lw_ref, k_ref, v_ref, kk_ref, b_ref, g_ref, rk_ref, lnw_ref, lnb_ref,
                o_ref, s_ref):
    @pl.when(pl.program_id(1) == 0)
    def _():
        s_ref[...] = jnp.zeros_like(s_ref)

    nb, chunk, d = r_ref.shape
    pairs = d // LANES
    pr = 2 * chunk
    row = lax.broadcasted_iota(jnp.int32, (pr, pr), 0)
    col = lax.broadcasted_iota(jnp.int32, (pr, pr), 1)
    same_head = (row // chunk) == (col // chunk)
    incl = jnp.logical_and(same_head, row >= col)
    incl2 = jnp.concatenate([incl, incl], axis=1)
    strict = jnp.logical_and(same_head, row > col)
    crow = lax.broadcasted_iota(jnp.int32, (chunk, chunk), 0)
    ccol = lax.broadcasted_iota(jnp.int32, (chunk, chunk), 1)
    tri = jnp.where(crow >= ccol, 1.0, 0.0).astype(BF16)
    lane = lambda p: slice(p * LANES, (p + 1) * LANES)

    def block_diag(x, p):
        xp = x[:, lane(p)]
        return jnp.where(same_head, jnp.concatenate([xp, xp], axis=0), 0.0).astype(BF16)

    def stack_heads(x):
        return jnp.concatenate([x[:, :HEAD_DIM], x[:, HEAD_DIM:]], axis=0)

    ar, bk, bke, vs, p_end = [], [], [], [], []
    for bi in range(nb):
        lw = lw_ref[bi]
        lp = sum(jnp.dot(tri, part, preferred_element_type=F32) for part in _split3(lw))
        lp_end = lp[chunk - 1:chunk, :]
        r = r_ref[bi].astype(F32)
        k = k_ref[bi].astype(F32)
        kk = kk_ref[bi].astype(F32)
        b = b_ref[bi].astype(F32)
        p_inv = jnp.exp(-lp)
        to_end = jnp.exp(lp_end - lp)
        a_t = -kk * jnp.exp(lp - lw)
        b_t = b * p_inv
        k_t = k * p_inv
        r_t = r * jnp.exp(lp)
        b_e = b * to_end
        k_e = k * to_end
        v_b = v_ref[bi]
        for p in range(pairs):
            ar.append(jnp.concatenate([block_diag(a_t, p), block_diag(r_t, p)], axis=0))
            bk.append(jnp.concatenate([block_diag(b_t, p), block_diag(k_t, p)], axis=0))
            bke.append(jnp.concatenate([block_diag(b_e, p), block_diag(k_e, p)], axis=0))
            vs.append(stack_heads(v_b[:, lane(p)]))
            p_end.append(jnp.exp(lp_end[:, lane(p)]))

    qs = range(nb * pairs)
    s = [s_ref[q] for q in qs]
    gram = [_dot_nt(ar[q], bk[q]) for q in qs]
    ars = [_dot_nt(ar[q], s[q]) for q in qs]
    a_ab = [jnp.where(strict, gram[q][:pr, :pr], 0.0) for q in qs]
    a_ak = [jnp.where(strict, gram[q][:pr, pr:], 0.0) for q in qs]
    r_bk = [jnp.where(incl2, gram[q][pr:, :], 0.0).astype(BF16) for q in qs]
    w_in = [ars[q][:pr] + _dot(a_ak[q], vs[q]) for q in qs]
    t_inv = _unit_lower_inverses(a_ab, row, col)
    u = [_dot(t_inv[q], w_in[q]) for q in qs]
    uv = [jnp.concatenate([u[q].astype(BF16), vs[q]], axis=0) for q in qs]
    y = [ars[q][pr:] + jnp.dot(r_bk[q], uv[q], preferred_element_type=F32) for q in qs]
    for q in qs:
        s_ref[q] = s[q] * p_end[q] + _dot_tn(uv[q], bke[q])

    def head_norm(yh):
        mean = jnp.mean(yh, axis=-1, keepdims=True)
        yc = yh - mean
        var = jnp.mean(yc * yc, axis=-1, keepdims=True)
        return yc * lax.rsqrt(var + LNX_EPS)

    left = lax.broadcasted_iota(jnp.int32, (chunk, LANES), 1) < HEAD_DIM
    for bi in range(nb):
        rkk = r_ref[bi].astype(F32) * k_ref[bi].astype(F32) * rk_ref[...]
        yn, dots = [], []
        for p in range(pairs):
            yq = y[bi * pairs + p]
            yn.append(jnp.concatenate([head_norm(yq[:chunk]), head_norm(yq[chunk:])], axis=-1))
            rk_p = rkk[:, lane(p)]
            dots.append(jnp.where(left, jnp.sum(rk_p[:, :HEAD_DIM], axis=-1, keepdims=True),
                                  jnp.sum(rk_p[:, HEAD_DIM:], axis=-1, keepdims=True)))
        yn = jnp.concatenate(yn, axis=-1)
        bonus = jnp.concatenate(dots, axis=-1) * v_ref[bi].astype(F32)
        o_ref[bi] = ((yn * lnw_ref[...] + lnb_ref[...] + bonus) * g_ref[bi].astype(F32)).astype(BF16)


def _wkv(r, lw, k, v, kk, b, g, r_k, lnx_w, lnx_b, seq):
    n, d = r.shape
    batch = n // seq
    nb = WKV_BATCH_ROWS
    assert batch % nb == 0
    view = lambda a: a.reshape(batch, seq, d)
    blk = pl.BlockSpec((nb, WKV_CHUNK, d), lambda bi, c: (bi, c, 0))
    vec = pl.BlockSpec((1, d), lambda bi, c: (0, 0))
    row2 = lambda p: p.reshape(1, d).astype(F32)
    out = pl.pallas_call(
        _wkv_kernel,
        grid=(batch // nb, seq // WKV_CHUNK),
        in_specs=[blk] * 7 + [vec] * 3,
        out_specs=blk,
        out_shape=jax.ShapeDtypeStruct((batch, seq, d), BF16),
        scratch_shapes=[pltpu.VMEM((nb * (d // LANES), HEAD_DIM, LANES), F32)],
        compiler_params=_params("arbitrary", "arbitrary"),
        name="wkv7",
    )(view(r), view(lw), view(k), view(v), view(kk), view(b), view(g), row2(r_k), row2(lnx_w), row2(lnx_b))
    return out.reshape(n, d)


def _wo_ffn_kernel(h_ref, yg_ref, wo_ref, gn_ref, wg_ref, wu_ref, wd_ref, o_ref, h1_s, hn_s, acc_s):
    f = pl.program_id(1)

    @pl.when(f == 0)
    def _():
        h1 = h_ref[...] + jnp.dot(yg_ref[...], wo_ref[...], preferred_element_type=F32)
        h1_s[...] = h1
        hn_s[...] = _rms(h1, gn_ref[...]).astype(BF16)
        acc_s[...] = jnp.zeros_like(acc_s)

    x = hn_s[...]
    gate = jnp.dot(x, wg_ref[...], preferred_element_type=F32)
    up = jnp.dot(x, wu_ref[...], preferred_element_type=F32)
    act = (gate * _sigmoid(gate) * up).astype(BF16)
    acc_s[...] += jnp.dot(act, wd_ref[...], preferred_element_type=F32)

    @pl.when(f == pl.num_programs(1) - 1)
    def _():
        o_ref[...] = h1_s[...] + acc_s[...]


def _wo_ffn(h, yg, w_o, gn, w_gate, w_up, w_down):
    n, d = h.shape
    ff = w_gate.shape[1]
    tm, tf = FFN_ROWS, FFN_COLS
    rows = pl.BlockSpec((tm, d), lambda i, f: (i, 0))
    return pl.pallas_call(
        _wo_ffn_kernel,
        grid=(n // tm, ff // tf),
        in_specs=[rows, rows, _full((d, d)), _full((1, d)),
                  pl.BlockSpec((d, tf), lambda i, f: (0, f)),
                  pl.BlockSpec((d, tf), lambda i, f: (0, f)),
                  pl.BlockSpec((tf, d), lambda i, f: (f, 0))],
        out_specs=rows,
        out_shape=jax.ShapeDtypeStruct((n, d), F32),
        scratch_shapes=[pltpu.VMEM((tm, d), F32), pltpu.VMEM((tm, d), BF16), pltpu.VMEM((tm, d), F32)],
        compiler_params=_params("arbitrary", "arbitrary"),
        name="wo_ffn",
    )(h, yg, w_o.astype(BF16), gn.reshape(1, d).astype(F32),
      w_gate.astype(BF16), w_up.astype(BF16), w_down.astype(BF16))


def _s5_disc_kernel(lr_ref, li_ref, dt_ref, bre_ref, bim_ref, are_o, aim_o, bbre_o, bbim_o):
    lr = jnp.minimum(lr_ref[...], S5_EIG_MAX)
    li = li_ref[...]
    dt = dt_ref[...]
    dr, di = lr * dt, li * dt
    e = jnp.exp(dr)
    abar_re = e * jnp.cos(di)
    abar_im = e * jnp.sin(di)
    sh = jnp.sin(0.5 * di)
    num_re = (e - 1.0) * jnp.cos(di) - 2.0 * sh * sh
    den = lr * lr + li * li
    coef_re = (num_re * lr + abar_im * li) / den
    coef_im = (abar_im * lr - num_re * li) / den
    are_o[...] = abar_re
    aim_o[...] = abar_im
    bre, bim = bre_ref[...], bim_ref[...]
    bbre_o[...] = coef_re * bre - coef_im * bim
    bbim_o[...] = coef_re * bim + coef_im * bre


def _s5_discretise(lam_re, lam_im, log_step, b_re, b_im):
    groups, states = lam_re.shape
    q = b_re.shape[-1]
    gp = groups * states
    col = lambda a: a.reshape(gp, 1).astype(F32)
    dt = jnp.exp(log_step.astype(F32))
    dt_col = jnp.broadcast_to(dt[:, None], (groups, states)).reshape(gp, 1)
    one = jax.ShapeDtypeStruct((gp, 1), F32)
    wide = jax.ShapeDtypeStruct((gp, q), F32)
    return pl.pallas_call(
        _s5_disc_kernel,
        out_shape=[one, one, wide, wide],
        name="s5_discretise",
    )(col(lam_re), col(lam_im), dt_col, b_re.reshape(gp, q).astype(F32), b_im.reshape(gp, q).astype(F32))


def _s5_glu_kernel(h_ref, gn_ref, wb_ref, are_ref, aim_ref, wcre_ref, wcim_ref, dsk_ref, wglu_ref, bglu_ref,
                   o_ref, st_s, xt_s, bu_s, y_s):
    @pl.when(pl.program_id(0) == 0)
    def _():
        st_s[...] = jnp.zeros_like(st_s)

    batch, steps, d = h_ref.shape
    slabs = d // LANES
    half = st_s.shape[-1] // 2
    lane = lambda s: slice(s * LANES, (s + 1) * LANES)
    gn = gn_ref[...]
    dsk = dsk_ref[...]

    for b in range(batch):
        hn_b = _rms(h_ref[b], gn)
        for s in range(slabs):
            xt_s[s, pl.ds(b, steps, stride=batch), :] = hn_b[:, lane(s)]

    def in_proj(s):
        bu_s[s % 2] = jnp.dot(xt_s[s].astype(BF16), wb_ref[s], preferred_element_type=F32)

    in_proj(0)
    for s in range(slabs):
        if s + 1 < slabs:
            in_proj(s + 1)
        sl = lane(s)
        buf = s % 2
        a_re = are_ref[s]
        a_im = aim_ref[s]
        h_re, h_im = st_s[s, :, :half], st_s[s, :, half:]
        for t in range(steps):
            rows = slice(t * batch, (t + 1) * batch)
            h_re, h_im = (a_re * h_re - a_im * h_im + bu_s[buf, rows, :half],
                          a_re * h_im + a_im * h_re + bu_s[buf, rows, half:])
            bu_s[buf, rows, :half] = h_re
            bu_s[buf, rows, half:] = h_im
        st_s[s, :, :half] = h_re
        st_s[s, :, half:] = h_im
        y = (jnp.dot(bu_s[buf, :, :half].astype(BF16), wcre_ref[s], preferred_element_type=F32)
             - jnp.dot(bu_s[buf, :, half:].astype(BF16), wcim_ref[s], preferred_element_type=F32))
        y = y + dsk[:, sl] * xt_s[s]
        y = 0.5 * y * (1.0 + jnp.tanh(math.sqrt(2.0 / math.pi) * (y + 0.044715 * (y * y * y))))
        y_s[:, sl] = y.astype(BF16)

    z = jnp.dot(y_s[...], wglu_ref[...], preferred_element_type=F32) + bglu_ref[...]
    glu = z[:, :d] * _sigmoid(z[:, d:])
    for s in range(slabs):
        xt_s[s] = glu[:, lane(s)]
    for b in range(batch):
        for s in range(slabs):
            o_ref[b, :, lane(s)] = h_ref[b, :, lane(s)] + xt_s[s, pl.ds(b, steps, stride=batch), :]


def _s5_glu(h, seq, gn, lam_re, lam_im, log_step, b_re, b_im, c_re, c_im, d_skip, w_glu, b_glu):
    n, d = h.shape
    groups, states = lam_re.shape
    q = S5_GROUP
    slabs = d // LANES
    gl = LANES // q
    half = gl * states
    abar_re, abar_im, bbar_re, bbar_im = _s5_discretise(lam_re, lam_im, log_step, b_re, b_im)
    eye = jnp.eye(gl, dtype=F32)

    def in_proj(bb):
        bb = bb.reshape(slabs, gl, states, q)
        return jnp.einsum('sgpc,gh->sgchp', bb, eye).reshape(slabs, LANES, half)

    def out_proj(c):
        c = c.astype(F32).reshape(slabs, gl, q, states)
        return jnp.einsum('sgcp,gh->sgphc', c, eye).reshape(slabs, half, LANES).astype(BF16)

    wb = jnp.concatenate([in_proj(bbar_re), in_proj(bbar_im)], axis=-1).astype(BF16)
    lanes_of = lambda a: jnp.broadcast_to(a.reshape(slabs, 1, half), (slabs, SUBLANES, half))
    batch = n // seq
    assert batch == SUBLANES
    steps = S5_ROWS // batch
    tm = S5_ROWS
    rows = pl.BlockSpec((batch, steps, d), lambda i: (0, i, 0))
    args = [h.reshape(batch, seq, d), gn.reshape(1, d).astype(F32), wb, lanes_of(abar_re), lanes_of(abar_im),
            out_proj(c_re), out_proj(c_im), d_skip.reshape(1, d).astype(F32),
            w_glu.astype(BF16), b_glu.reshape(1, 2 * d).astype(F32)]
    out = pl.pallas_call(
        _s5_glu_kernel,
        grid=(seq // steps,),
        in_specs=[rows] + [_full(a.shape) for a in args[1:]],
        out_specs=rows,
        out_shape=jax.ShapeDtypeStruct((batch, seq, d), F32),
        scratch_shapes=[pltpu.VMEM((slabs, batch, 2 * half), F32),
                        pltpu.VMEM((slabs, tm, LANES), F32),
                        pltpu.VMEM((2, tm, 2 * half), F32),
                        pltpu.VMEM((tm, d), BF16)],
        compiler_params=_params("arbitrary"),
        name="s5_glu",
    )(*args)
    return out.reshape(n, d)


def _router_kernel(h_ref, gn_ref, rhi_ref, rlo_ref, xn_o, meta_o, gate_o, cnt_o, base_s):
    @pl.when(pl.program_id(0) == 0)
    def _():
        base_s[...] = jnp.zeros_like(base_s)

    tm = h_ref.shape[0]
    hn = _rms(h_ref[...], gn_ref[...])
    xn_o[...] = hn
    hi, lo = _split2(hn)
    logits = (jnp.dot(hi, rhi_ref[...], preferred_element_type=F32)
              + jnp.dot(hi, rlo_ref[...], preferred_element_type=F32)
              + jnp.dot(lo, rhi_ref[...], preferred_element_type=F32))
    lane = lax.broadcasted_iota(jnp.int32, (tm, LANES), 1).astype(F32)
    logits = jnp.where(lane < N_EXPERTS, logits, NEG_BIG)
    m1 = jnp.max(logits, axis=-1, keepdims=True)
    i1 = jnp.min(jnp.where(logits == m1, lane, float(LANES)), axis=-1, keepdims=True)
    rest = jnp.where(lane == i1, NEG_BIG, logits)
    m2 = jnp.max(rest, axis=-1, keepdims=True)
    i2 = jnp.min(jnp.where(rest == m2, lane, float(LANES)), axis=-1, keepdims=True)
    e = jnp.exp(m2 - m1)
    g1 = 1.0 / (1.0 + e)
    g2 = e / (1.0 + e)

    oh1 = jnp.where(lane == i1, 1.0, 0.0)
    oh2 = jnp.where(lane == i2, 1.0, 0.0)
    both = oh1 + oh2
    row = lax.broadcasted_iota(jnp.int32, (tm, tm), 0)
    col = lax.broadcasted_iota(jnp.int32, (tm, tm), 1)
    before = jnp.where(row > col, 1.0, 0.0).astype(BF16)
    seen = jnp.dot(before, both.astype(BF16), preferred_element_type=F32) + base_s[...]
    rank1 = jnp.sum(oh1 * seen, axis=-1, keepdims=True)
    rank2 = jnp.sum(oh2 * seen, axis=-1, keepdims=True)
    base = base_s[...] + jnp.sum(both, axis=0, keepdims=True)
    base_s[...] = base
    cnt_o[...] = jnp.broadcast_to(base, cnt_o.shape)

    meta = jnp.where(lane == 0.0, i1, jnp.where(lane == 1.0, i2, jnp.where(lane == 2.0, rank1, rank2)))
    meta_o[...] = meta.astype(jnp.int32)
    gate_o[...] = jnp.where(lane == 0.0, g1, g2)


def _router(h, gn, router):
    n, d = h.shape
    tm = ROUTER_ROWS
    rp = _pad_cols(router.astype(F32), LANES)
    rhi = rp.astype(BF16)
    rlo = (rp - rhi.astype(F32)).astype(BF16)
    rows = pl.BlockSpec((tm, d), lambda i: (i, 0))
    narrow = pl.BlockSpec((tm, LANES), lambda i: (i, 0))
    return pl.pallas_call(
        _router_kernel,
        grid=(n // tm,),
        in_specs=[rows, _full((1, d)), _full((d, LANES)), _full((d, LANES))],
        out_specs=[rows, narrow, narrow, _full((SUBLANES, LANES))],
        out_shape=[jax.ShapeDtypeStruct((n, d), F32),
                   jax.ShapeDtypeStruct((n, LANES), jnp.int32),
                   jax.ShapeDtypeStruct((n, LANES), F32),
                   jax.ShapeDtypeStruct((SUBLANES, LANES), F32)],
        scratch_shapes=[pltpu.VMEM((1, LANES), F32)],
        compiler_params=_params("arbitrary"),
        name="moe_router",
    )(h, gn.reshape(1, d).astype(F32), rhi, rlo)


def _row_copy(src_ref, src_row, dst_ref, dst_row, sem):
    return pltpu.make_async_copy(src_ref.at[pl.ds(src_row, 1)], dst_ref.at[pl.ds(dst_row, 1)], sem)


def _slot_maps_kernel(slot_ref, seg_ref, src_o, dst_o, *, n, tm):
    total = src_o.shape[0]
    n_tiles = total // tm - 1

    def fill(first, last, shift):
        def body(s, carry):
            src_o[s] = 0
            dst_o[tm + s] = s + shift
            return carry
        lax.fori_loop(first, last, body, 0)

    def spare_shift(tile):
        return 2 * n + (tile % 3) * tm - tile * tm

    for e in range(N_EXPERTS):
        fill(seg_ref[e], seg_ref[N_EXPERTS + e], seg_ref[2 * N_EXPERTS + e])

    def idle_tile(t, carry):
        fill(t * tm, (t + 1) * tm, spare_shift(t))
        return carry
    lax.fori_loop(seg_ref[3 * N_EXPERTS], n_tiles, idle_tile, 0)

    def lead(s, carry):
        dst_o[s] = s + spare_shift(2) + 2 * tm
        src_o[total - tm + s] = 0
        return carry
    lax.fori_loop(0, tm, lead, 0, unroll=8)

    def place(token, carry):
        for k in range(2):
            s = slot_ref[2 * token + k]
            src_o[s] = token
            dst_o[tm + s] = k * n + token
        return carry
    lax.fori_loop(0, n, place, 0, unroll=8)


def _slot_maps(slot, pad_start, cnt, pad_end, n, n_tiles):
    tm = MOE_ROWS
    total = (n_tiles + 1) * tm
    last_tile = (pad_end - 1) // tm
    shift = 2 * n + (last_tile % 3) * tm - last_tile * tm
    segments = jnp.concatenate([pad_start + cnt, pad_end, shift, pad_end[-1:] // tm]).astype(jnp.int32)
    smem = pl.BlockSpec(memory_space=pltpu.SMEM)
    return pl.pallas_call(
        functools.partial(_slot_maps_kernel, n=n, tm=tm),
        in_specs=[smem, smem],
        out_specs=[smem, smem],
        out_shape=[jax.ShapeDtypeStruct((total,), jnp.int32)] * 2,
        compiler_params=pltpu.CompilerParams(disable_bounds_checks=True),
        name="moe_slot_maps",
    )(slot, segments)


def _experts_kernel(te_ref, nu_ref, src_ref, dst_ref, xn_ref, wg_ref, wu_ref, wd_ref, out_ref,
                    xg_s, xb_s, acc_s, ob_s, gsem, osem, *, n_cols):
    del te_ref
    i = pl.program_id(0)
    f = pl.program_id(1)
    n_used = nu_ref[0]
    used = i < n_used
    tm = acc_s.shape[0]
    per_step = tm // n_cols
    slot = i % 2
    other = 1 - slot

    def gather_row(tile, buf, r):
        _row_copy(xn_ref, src_ref[tile * tm + r], xg_s.at[buf], r, gsem.at[buf]).start(priority=1)

    def scatter_row(tile, buf, r):
        _row_copy(ob_s.at[buf], r, out_ref, dst_ref[(tile + 1) * tm + r], osem.at[buf]).start(priority=0)

    def wait_gather(buf):
        pltpu.make_async_copy(xn_ref.at[pl.ds(0, tm)], xg_s.at[buf], gsem.at[buf]).wait()

    def wait_scatter(buf):
        pltpu.make_async_copy(ob_s.at[buf], out_ref.at[pl.ds(0, tm)], osem.at[buf]).wait()

    @pl.when(jnp.logical_and(used, f == 0))
    def _():
        @pl.when(i == 0)
        def _():
            def body(r, carry):
                gather_row(0, 0, r)
                return carry
            lax.fori_loop(0, tm, body, 0, unroll=8)
            ob_s[...] = jnp.zeros_like(ob_s)
            spare0 = out_ref.shape[0] - 3 * tm
            for k in range(3):
                pltpu.make_async_copy(ob_s.at[0], out_ref.at[pl.ds(spare0 + k * tm, tm)], osem.at[0]).start()
            for k in range(3):
                wait_scatter(0)

        wait_gather(slot)
        xb_s[...] = xg_s[slot].astype(BF16)
        acc_s[...] = jnp.zeros_like(acc_s)

    @pl.when(used)
    def _():
        for u in range(per_step):
            r = f * per_step + u
            gather_row(i + 1, other, r)
            scatter_row(i - 1, other, r)
        x = xb_s[...]
        gate = jnp.dot(x, wg_ref[0, 0].astype(BF16), preferred_element_type=F32)
        up = jnp.dot(x, wu_ref[0, 0].astype(BF16), preferred_element_type=F32)
        act = (gate * _sigmoid(gate) * up).astype(BF16)
        acc_s[...] += jnp.dot(act, wd_ref[0, 0].astype(BF16), preferred_element_type=F32)

    @pl.when(jnp.logical_and(used, f == n_cols - 1))
    def _():
        @pl.when(i > 0)
        def _():
            wait_scatter(slot)

        ob_s[slot] = acc_s[...]

        @pl.when(i == n_used - 1)
        def _():
            def body(r, carry):
                scatter_row(i, slot, r)
                return carry
            lax.fori_loop(0, tm, body, 0, unroll=8)
            wait_scatter(other)
            wait_scatter(slot)
            wait_gather(other)


def _experts(xn, tile_expert, n_used, src, dst, layer, w_gate, w_up, w_down):
    n, d = xn.shape
    fe = w_gate.shape[-1]
    tm, tf = MOE_ROWS, MOE_COLS
    nf = fe // tf
    assert tm % nf == 0
    n_tiles = tile_expert.shape[0]
    col = lambda i, f, nu: jnp.where(i < nu[0], f, nf - 1)
    return pl.pallas_call(
        functools.partial(_experts_kernel, n_cols=nf),
        grid_spec=pltpu.PrefetchScalarGridSpec(
            num_scalar_prefetch=4,
            grid=(n_tiles, nf),
            in_specs=[pl.BlockSpec(memory_space=pl.ANY),
                      pl.BlockSpec((1, 1, d, tf), lambda i, f, te, nu, s, t: (layer, te[i], 0, col(i, f, nu))),
                      pl.BlockSpec((1, 1, d, tf), lambda i, f, te, nu, s, t: (layer, te[i], 0, col(i, f, nu))),
                      pl.BlockSpec((1, 1, tf, d), lambda i, f, te, nu, s, t: (layer, te[i], col(i, f, nu), 0))],
            out_specs=pl.BlockSpec(memory_space=pl.ANY),
            scratch_shapes=[pltpu.VMEM((2, tm, d), F32), pltpu.VMEM((tm, d), BF16),
                            pltpu.VMEM((tm, d), F32), pltpu.VMEM((2, tm, d), F32),
                            pltpu.SemaphoreType.DMA((2,)), pltpu.SemaphoreType.DMA((2,))],
        ),
        out_shape=jax.ShapeDtypeStruct((2 * n + 3 * tm, d), F32),
        compiler_params=_params("arbitrary", "arbitrary"),
        name="moe_experts",
    )(tile_expert, n_used, src, dst, xn, w_gate, w_up, w_down)


def _combine_kernel(h_ref, gate_ref, fn_ref, e0_ref, e1_ref, o_ref, *, final_norm):
    gates = gate_ref[...]
    out = h_ref[...] + gates[:, 0:1] * e0_ref[...] + gates[:, 1:2] * e1_ref[...]
    if final_norm:
        out = _rms(out, fn_ref[...])
    o_ref[...] = out


def _combine(h, gates, eo, final_gain):
    n, d = h.shape
    tm = FFN_ROWS
    final_norm = final_gain is not None
    fn = (final_gain if final_norm else jnp.ones((d,), F32)).reshape(1, d).astype(F32)
    rows = pl.BlockSpec((tm, d), lambda i: (i, 0))
    return pl.pallas_call(
        functools.partial(_combine_kernel, final_norm=final_norm),
        grid=(n // tm,),
        in_specs=[rows, pl.BlockSpec((tm, LANES), lambda i: (i, 0)), _full((1, d)),
                  rows, pl.BlockSpec((tm, d), lambda i: (i + n // tm, 0))],
        out_specs=rows,
        out_shape=jax.ShapeDtypeStruct((n, d), F32),
        compiler_params=_params("arbitrary"),
        name="moe_combine",
    )(h, gates, fn, eo, eo)


def _moe(h, gn, router, layer, w_gate, w_up, w_down, final_gain):
    n, d = h.shape
    tm = MOE_ROWS
    xn, meta, gates, counts = _router(h, gn, router)
    n_tiles = -(-2 * n // tm) + N_EXPERTS
    cnt = counts[0, :N_EXPERTS].astype(jnp.int32)
    padded = (cnt + tm - 1) // tm * tm
    pad_end = jnp.cumsum(padded)
    pad_start = pad_end - padded
    slot = (pad_start[meta[:, 0:2]] + meta[:, 2:4]).reshape(-1).astype(jnp.int32)
    tile_start = jnp.arange(n_tiles, dtype=jnp.int32) * tm
    tile_expert = jnp.minimum(jnp.sum(tile_start[:, None] >= pad_end[None, :], axis=1),
                              N_EXPERTS - 1).astype(jnp.int32)
    n_used = (pad_end[-1:] // tm).astype(jnp.int32)
    src, dst = _slot_maps(slot, pad_start, cnt, pad_end, n, n_tiles)
    eo = _experts(xn, tile_expert, n_used, src, dst, layer, w_gate, w_up, w_down)
    return _combine(h, gates, eo, final_gain)


def kernel(x, norm_mix, norm_ffn, final_norm, rwkv_mu, rwkv_w_rkv, rwkv_w0, rwkv_w1, rwkv_w2, rwkv_a0, rwkv_a1, rwkv_a2, rwkv_g1, rwkv_g2, rwkv_k_k, rwkv_k_a, rwkv_r_k, rwkv_lnx_w, rwkv_lnx_b, rwkv_w_o, rwkv_v0, rwkv_v1, rwkv_v2, s5_lam_re, s5_lam_im, s5_log_step, s5_b_re, s5_b_im, s5_c_re, s5_c_im, s5_d, s5_w_glu, s5_b_glu, ffn_gate, ffn_up, ffn_down, moe_router, moe_gate, moe_up, moe_down):
    batch, seq, d = x.shape
    depth = norm_mix.shape[0]
    assert d % LANES == 0 and seq % WKV_CHUNK == 0
    n = batch * seq
    h = x.astype(F32).reshape(n, d)
    v_first = None
    for i in range(depth):
        j = i // 2
        last = i == depth - 1
        if i % 2 == 0:
            v_mix = None if j == 0 else (rwkv_v0[j - 1], rwkv_v1[j - 1], rwkv_v2[j - 1])
            r, lw, k, v, kk, b, g = _rwkv_proj(
                h, seq, norm_mix[i], rwkv_mu[j], rwkv_w_rkv[j], rwkv_w0[j], rwkv_w1[j], rwkv_w2[j],
                rwkv_a0[j], rwkv_a1[j], rwkv_a2[j], rwkv_g1[j], rwkv_g2[j], rwkv_k_k[j], rwkv_k_a[j],
                v_first, v_mix)
            if v_first is None:
                v_first = v
            yg = _wkv(r, lw, k, v, kk, b, g, rwkv_r_k[j], rwkv_lnx_w[j], rwkv_lnx_b[j], seq)
            h = _wo_ffn(h, yg, rwkv_w_o[j], norm_ffn[i], ffn_gate[j], ffn_up[j], ffn_down[j])
            if last:
                h = _final_norm(h, final_norm)
        else:
            h = _s5_glu(h, seq, norm_mix[i], s5_lam_re[j], s5_lam_im[j], s5_log_step[j], s5_b_re[j], s5_b_im[j],
                        s5_c_re[j], s5_c_im[j], s5_d[j], s5_w_glu[j], s5_b_glu[j])
            h = _moe(h, norm_ffn[i], moe_router[j], j, moe_gate, moe_up, moe_down,
                     final_norm if last else None)
    return h.reshape(batch, seq, d).astype(x.dtype)


def _final_norm_kernel(h_ref, g_ref, o_ref):
    o_ref[...] = _rms(h_ref[...], g_ref[...])


def _final_norm(h, gain):
    n, d = h.shape
    rows = pl.BlockSpec((FFN_ROWS, d), lambda i: (i, 0))
    return pl.pallas_call(
        _final_norm_kernel, grid=(n // FFN_ROWS,), in_specs=[rows, _full((1, d))], out_specs=rows,
        out_shape=jax.ShapeDtypeStruct((n, d), F32), compiler_params=_params("arbitrary"),
        name="final_norm",
    )(h, gain.reshape(1, d).astype(F32))
```

```python
import functools
import math

import jax
import jax.numpy as jnp
from jax import lax
from jax.experimental import pallas as pl
from jax.experimental.pallas import tpu as pltpu

F32 = jnp.float32
BF16 = jnp.bfloat16

NORM_EPS = 1e-6
LNX_EPS = 64e-5
HEAD_DIM = 64
S5_GROUP = 16
S5_STATE = 64
S5_EIG_MAX = -1e-4
N_EXPERTS = 8
LANES = 128
SUBLANES = 8
WKV_CHUNK = 64
WKV_BATCH_ROWS = 4
VMEM_LIMIT = 56 * 1024 * 1024

PROJ_ROWS = 512
FFN_ROWS = 512
FFN_COLS = 2816
S5_ROWS = 512
ROUTER_ROWS = 512
MOE_ROWS = 768
MOE_COLS = 896
NEG_BIG = -1e30


def _params(*semantics):
    return pltpu.CompilerParams(dimension_semantics=semantics, vmem_limit_bytes=VMEM_LIMIT)


def _dot(a, b):
    return jnp.dot(a.astype(BF16), b.astype(BF16), preferred_element_type=F32)


def _dot_nt(a, b):
    return lax.dot_general(a.astype(BF16), b.astype(BF16), (((1,), (1,)), ((), ())),
                           preferred_element_type=F32)


def _dot_tn(a, b):
    return lax.dot_general(a.astype(BF16), b.astype(BF16), (((0,), (0,)), ((), ())),
                           preferred_element_type=F32)


def _split2(x):
    hi = x.astype(BF16)
    lo = (x - hi.astype(F32)).astype(BF16)
    return hi, lo


def _split3(x):
    hi = x.astype(BF16)
    r1 = x - hi.astype(F32)
    mid = r1.astype(BF16)
    lo = (r1 - mid.astype(F32)).astype(BF16)
    return hi, mid, lo


def _rms(x, g):
    return x * lax.rsqrt(jnp.mean(x * x, axis=-1, keepdims=True) + NORM_EPS) * g


def _sigmoid(x):
    return 1.0 / (1.0 + jnp.exp(-x))


def _full(shape):
    n = len(shape)
    return pl.BlockSpec(shape, lambda *_: (0,) * n)


def _rwkv_proj_kernel(*refs, has_vmix, tiles_per_seq):
    (x_ref, xp_ref, gn_ref, mu_ref, wrkv_ref, w0_ref, w1_ref, w2_ref, a0_ref, a1_ref, a2_ref,
     g1_ref, g2_ref, kk_ref, ka_ref, hs_ref, hst_ref) = refs[:17]
    rest = refs[17:]
    if has_vmix:
        vf_ref, v0_ref, v1_ref, v2_ref = rest[:4]
        rest = rest[4:]
    r_o, lw_o, k_o, v_o, kk_o, b_o, g_o = rest

    i = pl.program_id(0)
    gn = gn_ref[...]
    hn = _rms(x_ref[...], gn)
    starts_seq = (i % tiles_per_seq) == 0
    prev = _rms(xp_ref[SUBLANES - 1:SUBLANES, :], gn) * jnp.where(starts_seq, 0.0, 1.0)
    first_row = lax.broadcasted_iota(jnp.int32, hn.shape, 0) == 0
    shifted = jnp.where(first_row, prev, pltpu.roll(hn, 1, axis=0))
    xx = shifted - hn
    xr, xk, xv, xw, xa, xg = [(hn + xx * mu_ref[s:s + 1, :]).astype(BF16) for s in range(6)]

    r = jnp.dot(xr, wrkv_ref[0], preferred_element_type=F32)
    k = jnp.dot(xk, wrkv_ref[1], preferred_element_type=F32)
    v = jnp.dot(xv, wrkv_ref[2], preferred_element_type=F32)

    z = w0_ref[...] + _dot(jnp.tanh(_dot(xw, w1_ref[...])), w2_ref[...])
    lw_o[...] = -math.exp(-0.5) * _sigmoid(z)

    if has_vmix:
        mixv = _sigmoid(v0_ref[...] + _dot(_dot(xv, v1_ref[...]), v2_ref[...]))
        v = v + (vf_ref[...].astype(F32) - v) * mixv
    a = _sigmoid(a0_ref[...] + _dot(_dot(xa, a1_ref[...]), a2_ref[...]))
    g = _dot(_sigmoid(_dot(xg, g1_ref[...])), g2_ref[...])

    kk = k * kk_ref[...]
    hi, lo = _split2(kk * kk)
    ss = (jnp.dot(hi, hs_ref[...], preferred_element_type=F32)
          + jnp.dot(lo, hs_ref[...], preferred_element_type=F32))
    inv = 1.0 / jnp.maximum(jnp.sqrt(ss), 1e-12)
    hi, lo = _split2(inv)
    inv_full = (jnp.dot(hi, hst_ref[...], preferred_element_type=F32)
                + jnp.dot(lo, hst_ref[...], preferred_element_type=F32))
    kk = kk * inv_full
    k = k * (1.0 + (a - 1.0) * ka_ref[...])

    r_o[...] = r.astype(BF16)
    k_o[...] = k.astype(BF16)
    v_o[...] = v.astype(BF16)
    kk_o[...] = kk.astype(BF16)
    b_o[...] = (kk * a).astype(BF16)
    g_o[...] = g.astype(BF16)


def _pad_cols(w, n):
    return jnp.pad(w, ((0, 0), (0, n - w.shape[1])))


def _pad_rows(w, n):
    return jnp.pad(w, ((0, n - w.shape[0]), (0, 0)))


def _rwkv_proj(h, seq, gn, mu, w_rkv, w0, w1, w2, a0, a1, a2, g1, g2, k_k, k_a, v_first, v_mix):
    n, d = h.shape
    row2 = lambda p: p.reshape(1, d).astype(F32)
    lora = lambda down, up: (_pad_cols(down, _rup(down.shape[1])).astype(BF16),
                             _pad_rows(up, _rup(up.shape[0])).astype(BF16))
    w1p, w2p = lora(w1, w2)
    a1p, a2p = lora(a1, a2)
    g1p, g2p = lora(g1, g2)
    head_of = jnp.arange(d, dtype=jnp.int32) // HEAD_DIM
    hs = (head_of[:, None] == jnp.arange(LANES, dtype=jnp.int32)[None, :]).astype(BF16)
    hst = hs.T
    has_vmix = v_mix is not None
    tm = PROJ_ROWS
    assert seq % tm == 0
    row_spec = pl.BlockSpec((tm, d), lambda i: (i, 0))
    prev_spec = pl.BlockSpec((SUBLANES, d), lambda i: (jnp.maximum(i * (tm // SUBLANES) - 1, 0), 0))
    args = [h, h, row2(gn), mu.astype(F32), w_rkv.astype(BF16), row2(w0), w1p, w2p, row2(a0), a1p, a2p,
            g1p, g2p, row2(k_k), row2(k_a), hs, hst]
    specs = [row_spec, prev_spec] + [_full(a.shape) for a in args[2:]]
    if has_vmix:
        v0, v1, v2 = v_mix
        v1p, v2p = lora(v1, v2)
        extra = [v_first, row2(v0), v1p, v2p]
        args += extra
        specs += [row_spec] + [_full(a.shape) for a in extra[1:]]
    bf = jax.ShapeDtypeStruct((n, d), BF16)
    out_shape = [bf, jax.ShapeDtypeStruct((n, d), F32), bf, bf, bf, bf, bf]
    return pl.pallas_call(
        functools.partial(_rwkv_proj_kernel, has_vmix=has_vmix, tiles_per_seq=seq // tm),
        grid=(n // tm,),
        in_specs=specs,
        out_specs=[row_spec] * 7,
        out_shape=out_shape,
        compiler_params=_params("arbitrary"),
        name="rwkv_proj",
    )(*args)


def _rup(x, m=LANES):
    return (x + m - 1) // m * m


def _unit_lower_inverses(mats, row, col):
    same = lambda nb: (row // nb) == (col // nb)
    eye = jnp.where(row == col, 1.0, 0.0)
    n1 = [jnp.where(same(8), a, 0.0) for a in mats]
    n2 = [_dot(x, x) for x in n1]
    n4 = [_dot(x, x) for x in n2]
    t = [eye + x for x in n1]
    t = [ti + _dot(ti, x) for ti, x in zip(t, n2)]
    t = [ti + _dot(ti, x) for ti, x in zip(t, n4)]
    size = 8
    while size < WKV_CHUNK:
        grow = jnp.logical_and(same(2 * size), jnp.logical_not(same(size)))
        left = [_dot(ti, jnp.where(grow, a, 0.0)) for ti, a in zip(t, mats)]
        t = [ti + _dot(li, ti) for ti, li in zip(t, left)]
        size *= 2
    return t


def _wkv_kernel(r_ref, lw_ref, k_ref, v_ref, kk_ref, b_ref, g_ref, rk_ref, lnw_ref, lnb_ref,
                o_ref, s_ref):
    @pl.when(pl.program_id(1) == 0)
    def _():
        s_ref[...] = jnp.zeros_like(s_ref)

    nb, chunk, d = r_ref.shape
    pairs = d // LANES
    pr = 2 * chunk
    row = lax.broadcasted_iota(jnp.int32, (pr, pr), 0)
    col = lax.broadcasted_iota(jnp.int32, (pr, pr), 1)
    same_head = (row // chunk) == (col // chunk)
    incl = jnp.logical_and(same_head, row >= col)
    incl2 = jnp.concatenate([incl, incl], axis=1)
    strict = jnp.logical_and(same_head, row > col)
    crow = lax.broadcasted_iota(jnp.int32, (chunk, chunk), 0)
    ccol = lax.broadcasted_iota(jnp.int32, (chunk, chunk), 1)
    tri = jnp.where(crow >= ccol, 1.0, 0.0).astype(BF16)
    lane = lambda p: slice(p * LANES, (p + 1) * LANES)

    def block_diag(x, p):
        xp = x[:, lane(p)]
        return jnp.where(same_head, jnp.concatenate([xp, xp], axis=0), 0.0).astype(BF16)

    def stack_heads(x):
        return jnp.concatenate([x[:, :HEAD_DIM], x[:, HEAD_DIM:]], axis=0)

    ar, bk, bke, vs, p_end = [], [], [], [], []
    for bi in range(nb):
        lw = lw_ref[bi]
        lp = sum(jnp.dot(tri, part, preferred_element_type=F32) for part in _split3(lw))
        lp_end = lp[chunk - 1:chunk, :]
        r = r_ref[bi].astype(F32)
        k = k_ref[bi].astype(F32)
        kk = kk_ref[bi].astype(F32)
        b = b_ref[bi].astype(F32)
        p_inv = jnp.exp(-lp)
        to_end = jnp.exp(lp_end - lp)
        a_t = -kk * jnp.exp(lp - lw)
        b_t = b * p_inv
        k_t = k * p_inv
        r_t = r * jnp.exp(lp)
        b_e = b * to_end
        k_e = k * to_end
        v_b = v_ref[bi]
        for p in range(pairs):
            ar.append(jnp.concatenate([block_diag(a_t, p), block_diag(r_t, p)], axis=0))
            bk.append(jnp.concatenate([block_diag(b_t, p), block_diag(k_t, p)], axis=0))
            bke.append(jnp.concatenate([block_diag(b_e, p), block_diag(k_e, p)], axis=0))
            vs.append(stack_heads(v_b[:, lane(p)]))
            p_end.append(jnp.exp(lp_end[:, lane(p)]))

    qs = range(nb * pairs)
    s = [s_ref[q] for q in qs]
    gram = [_dot_nt(ar[q], bk[q]) for q in qs]
    ars = [_dot_nt(ar[q], s[q]) for q in qs]
    a_ab = [jnp.where(strict, gram[q][:pr, :pr], 0.0) for q in qs]
    a_ak = [jnp.where(strict, gram[q][:pr, pr:], 0.0) for q in qs]
    r_bk = [jnp.where(incl2, gram[q][pr:, :], 0.0).astype(BF16) for q in qs]
    w_in = [ars[q][:pr] + _dot(a_ak[q], vs[q]) for q in qs]
    t_inv = _unit_lower_inverses(a_ab, row, col)
    u = [_dot(t_inv[q], w_in[q]) for q in qs]
    uv = [jnp.concatenate([u[q].astype(BF16), vs[q]], axis=0) for q in qs]
    y = [ars[q][pr:] + jnp.dot(r_bk[q], uv[q], preferred_element_type=F32) for q in qs]
    for q in qs:
        s_ref[q] = s[q] * p_end[q] + _dot_tn(uv[q], bke[q])

    def head_norm(yh):
        mean = jnp.mean(yh, axis=-1, keepdims=True)
        yc = yh - mean
        var = jnp.mean(yc * yc, axis=-1, keepdims=True)
        return yc * lax.rsqrt(var + LNX_EPS)

    left = lax.broadcasted_iota(jnp.int32, (chunk, LANES), 1) < HEAD_DIM
    for bi in range(nb):
        rkk = r_ref[bi].astype(F32) * k_ref[bi].astype(F32) * rk_ref[...]
        yn, dots = [], []
        for p in range(pairs):
            yq = y[bi * pairs + p]
            yn.append(jnp.concatenate([head_norm(yq[:chunk]), head_norm(yq[chunk:])], axis=-1))
            rk_p = rkk[:, lane(p)]
            dots.append(jnp.where(left, jnp.sum(rk_p[:, :HEAD_DIM], axis=-1, keepdims=True),
                                  jnp.sum(rk_p[:, HEAD_DIM:], axis=-1, keepdims=True)))
        yn = jnp.concatenate(yn, axis=-1)
        bonus = jnp.concatenate(dots, axis=-1) * v_ref[bi].astype(F32)
        o_ref[bi] = ((yn * lnw_ref[...] + lnb_ref[...] + bonus) * g_ref[bi].astype(F32)).astype(BF16)


def _wkv(r, lw, k, v, kk, b, g, r_k, lnx_w, lnx_b, seq):
    n, d = r.shape
    batch = n // seq
    nb = WKV_BATCH_ROWS
    assert batch % nb == 0
    view = lambda a: a.reshape(batch, seq, d)
    blk = pl.BlockSpec((nb, WKV_CHUNK, d), lambda bi, c: (bi, c, 0))
    vec = pl.BlockSpec((1, d), lambda bi, c: (0, 0))
    row2 = lambda p: p.reshape(1, d).astype(F32)
    out = pl.pallas_call(
        _wkv_kernel,
        grid=(batch // nb, seq // WKV_CHUNK),
        in_specs=[blk] * 7 + [vec] * 3,
        out_specs=blk,
        out_shape=jax.ShapeDtypeStruct((batch, seq, d), BF16),
        scratch_shapes=[pltpu.VMEM((nb * (d // LANES), HEAD_DIM, LANES), F32)],
        compiler_params=_params("arbitrary", "arbitrary"),
        name="wkv7",
    )(view(r), view(lw), view(k), view(v), view(kk), view(b), view(g), row2(r_k), row2(lnx_w), row2(lnx_b))
    return out.reshape(n, d)


def _wo_ffn_kernel(h_ref, yg_ref, wo_ref, gn_ref, wg_ref, wu_ref, wd_ref, o_ref, h1_s, hn_s, acc_s):
    f = pl.program_id(1)

    @pl.when(f == 0)
    def _():
        h1 = h_ref[...] + jnp.dot(yg_ref[...], wo_ref[...], preferred_element_type=F32)
        h1_s[...] = h1
        hn_s[...] = _rms(h1, gn_ref[...]).astype(BF16)
        acc_s[...] = jnp.zeros_like(acc_s)

    x = hn_s[...]
    gate = jnp.dot(x, wg_ref[...], preferred_element_type=F32)
    up = jnp.dot(x, wu_ref[...], preferred_element_type=F32)
    act = (gate * _sigmoid(gate) * up).astype(BF16)
    acc_s[...] += jnp.dot(act, wd_ref[...], preferred_element_type=F32)

    @pl.when(f == pl.num_programs(1) - 1)
    def _():
        o_ref[...] = h1_s[...] + acc_s[...]


def _wo_ffn(h, yg, w_o, gn, w_gate, w_up, w_down):
    n, d = h.shape
    ff = w_gate.shape[1]
    tm, tf = FFN_ROWS, FFN_COLS
    rows = pl.BlockSpec((tm, d), lambda i, f: (i, 0))
    return pl.pallas_call(
        _wo_ffn_kernel,
        grid=(n // tm, ff // tf),
        in_specs=[rows, rows, _full((d, d)), _full((1, d)),
                  pl.BlockSpec((d, tf), lambda i, f: (0, f), pipeline_mode=pl.Buffered(1)),
                  pl.BlockSpec((d, tf), lambda i, f: (0, f), pipeline_mode=pl.Buffered(1)),
                  pl.BlockSpec((tf, d), lambda i, f: (f, 0), pipeline_mode=pl.Buffered(1))],
        out_specs=rows,
        out_shape=jax.ShapeDtypeStruct((n, d), F32),
        scratch_shapes=[pltpu.VMEM((tm, d), F32), pltpu.VMEM((tm, d), BF16), pltpu.VMEM((tm, d), F32)],
        compiler_params=_params("arbitrary", "arbitrary"),
        name="wo_ffn",
    )(h, yg, w_o.astype(BF16), gn.reshape(1, d).astype(F32),
      w_gate.astype(BF16), w_up.astype(BF16), w_down.astype(BF16))


def _s5_disc_kernel(lr_ref, li_ref, dt_ref, bre_ref, bim_ref, are_o, aim_o, bbre_o, bbim_o):
    lr = jnp.minimum(lr_ref[...], S5_EIG_MAX)
    li = li_ref[...]
    dt = dt_ref[...]
    dr, di = lr * dt, li * dt
    e = jnp.exp(dr)
    abar_re = e * jnp.cos(di)
    abar_im = e * jnp.sin(di)
    sh = jnp.sin(0.5 * di)
    num_re = (e - 1.0) * jnp.cos(di) - 2.0 * sh * sh
    den = lr * lr + li * li
    coef_re = (num_re * lr + abar_im * li) / den
    coef_im = (abar_im * lr - num_re * li) / den
    are_o[...] = abar_re
    aim_o[...] = abar_im
    bre, bim = bre_ref[...], bim_ref[...]
    bbre_o[...] = coef_re * bre - coef_im * bim
    bbim_o[...] = coef_re * bim + coef_im * bre


def _s5_discretise(lam_re, lam_im, log_step, b_re, b_im):
    groups, states = lam_re.shape
    q = b_re.shape[-1]
    gp = groups * states
    col = lambda a: a.reshape(gp, 1).astype(F32)
    dt = jnp.exp(log_step.astype(F32))
    dt_col = jnp.broadcast_to(dt[:, None], (groups, states)).reshape(gp, 1)
    one = jax.ShapeDtypeStruct((gp, 1), F32)
    wide = jax.ShapeDtypeStruct((gp, q), F32)
    return pl.pallas_call(
        _s5_disc_kernel,
        out_shape=[one, one, wide, wide],
        name="s5_discretise",
    )(col(lam_re), col(lam_im), dt_col, b_re.reshape(gp, q).astype(F32), b_im.reshape(gp, q).astype(F32))


def _s5_glu_kernel(h_ref, gn_ref, wb_ref, are_ref, aim_ref, wcre_ref, wcim_ref, dsk_ref, wglu_ref, bglu_ref,
                   o_ref, st_s, xt_s, bu_s, y_s):
    @pl.when(pl.program_id(0) == 0)
    def _():
        st_s[...] = jnp.zeros_like(st_s)

    batch, steps, d = h_ref.shape
    slabs = d // LANES
    half = st_s.shape[-1] // 2
    lane = lambda s: slice(s * LANES, (s + 1) * LANES)
    gn = gn_ref[...]
    dsk = dsk_ref[...]

    for b in range(batch):
        hn_b = _rms(h_ref[b], gn)
        for s in range(slabs):
            xt_s[s, pl.ds(b, steps, stride=batch), :] = hn_b[:, lane(s)]

    def in_proj(s):
        bu_s[s % 2] = jnp.dot(xt_s[s].astype(BF16), wb_ref[s], preferred_element_type=F32)

    in_proj(0)
    for s in range(slabs):
        if s + 1 < slabs:
            in_proj(s + 1)
        sl = lane(s)
        buf = s % 2
        a_re = are_ref[s]
        a_im = aim_ref[s]
        h_re, h_im = st_s[s, :, :half], st_s[s, :, half:]
        for t in range(steps):
            rows = slice(t * batch, (t + 1) * batch)
            h_re, h_im = (a_re * h_re - a_im * h_im + bu_s[buf, rows, :half],
                          a_re * h_im + a_im * h_re + bu_s[buf, rows, half:])
            bu_s[buf, rows, :half] = h_re
            bu_s[buf, rows, half:] = h_im
        st_s[s, :, :half] = h_re
        st_s[s, :, half:] = h_im
        y = (jnp.dot(bu_s[buf, :, :half].astype(BF16), wcre_ref[s], preferred_element_type=F32)
             - jnp.dot(bu_s[buf, :, half:].astype(BF16), wcim_ref[s], preferred_element_type=F32))
        y = y + dsk[:, sl] * xt_s[s]
        y = 0.5 * y * (1.0 + jnp.tanh(math.sqrt(2.0 / math.pi) * (y + 0.044715 * (y * y * y))))
        y_s[:, sl] = y.astype(BF16)

    z = jnp.dot(y_s[...], wglu_ref[...], preferred_element_type=F32) + bglu_ref[...]
    glu = z[:, :d] * _sigmoid(z[:, d:])
    for s in range(slabs):
        xt_s[s] = glu[:, lane(s)]
    for b in range(batch):
        for s in range(slabs):
            o_ref[b, :, lane(s)] = h_ref[b, :, lane(s)] + xt_s[s, pl.ds(b, steps, stride=batch), :]


def _s5_glu(h, seq, gn, lam_re, lam_im, log_step, b_re, b_im, c_re, c_im, d_skip, w_glu, b_glu):
    n, d = h.shape
    groups, states = lam_re.shape
    q = S5_GROUP
    slabs = d // LANES
    gl = LANES // q
    half = gl * states
    abar_re, abar_im, bbar_re, bbar_im = _s5_discretise(lam_re, lam_im, log_step, b_re, b_im)
    eye = jnp.eye(gl, dtype=F32)

    def in_proj(bb):
        bb = bb.reshape(slabs, gl, states, q)
        return jnp.einsum('sgpc,gh->sgchp', bb, eye).reshape(slabs, LANES, half)

    def out_proj(c):
        c = c.astype(F32).reshape(slabs, gl, q, states)
        return jnp.einsum('sgcp,gh->sgphc', c, eye).reshape(slabs, half, LANES).astype(BF16)

    wb = jnp.concatenate([in_proj(bbar_re), in_proj(bbar_im)], axis=-1).astype(BF16)
    lanes_of = lambda a: jnp.broadcast_to(a.reshape(slabs, 1, half), (slabs, SUBLANES, half))
    batch = n // seq
    assert batch == SUBLANES
    steps = S5_ROWS // batch
    tm = S5_ROWS
    rows = pl.BlockSpec((batch, steps, d), lambda i: (0, i, 0))
    args = [h.reshape(batch, seq, d), gn.reshape(1, d).astype(F32), wb, lanes_of(abar_re), lanes_of(abar_im),
            out_proj(c_re), out_proj(c_im), d_skip.reshape(1, d).astype(F32),
            w_glu.astype(BF16), b_glu.reshape(1, 2 * d).astype(F32)]
    out = pl.pallas_call(
        _s5_glu_kernel,
        grid=(seq // steps,),
        in_specs=[rows] + [_full(a.shape) for a in args[1:]],
        out_specs=rows,
        out_shape=jax.ShapeDtypeStruct((batch, seq, d), F32),
        scratch_shapes=[pltpu.VMEM((slabs, batch, 2 * half), F32),
                        pltpu.VMEM((slabs, tm, LANES), F32),
                        pltpu.VMEM((2, tm, 2 * half), F32),
                        pltpu.VMEM((tm, d), BF16)],
        compiler_params=_params("arbitrary"),
        name="s5_glu",
    )(*args)
    return out.reshape(n, d)


def _router_kernel(h_ref, gn_ref, rhi_ref, rlo_ref, xn_o, meta_o, gate_o, cnt_o, base_s):
    @pl.when(pl.program_id(0) == 0)
    def _():
        base_s[...] = jnp.zeros_like(base_s)

    tm = h_ref.shape[0]
    hn = _rms(h_ref[...], gn_ref[...])
    xn_o[...] = hn
    hi, lo = _split2(hn)
    logits = (jnp.dot(hi, rhi_ref[...], preferred_element_type=F32)
              + jnp.dot(hi, rlo_ref[...], preferred_element_type=F32)
              + jnp.dot(lo, rhi_ref[...], preferred_element_type=F32))
    lane = lax.broadcasted_iota(jnp.int32, (tm, LANES), 1).astype(F32)
    logits = jnp.where(lane < N_EXPERTS, logits, NEG_BIG)
    m1 = jnp.max(logits, axis=-1, keepdims=True)
    i1 = jnp.min(jnp.where(logits == m1, lane, float(LANES)), axis=-1, keepdims=True)
    rest = jnp.where(lane == i1, NEG_BIG, logits)
    m2 = jnp.max(rest, axis=-1, keepdims=True)
    i2 = jnp.min(jnp.where(rest == m2, lane, float(LANES)), axis=-1, keepdims=True)
    e = jnp.exp(m2 - m1)
    g1 = 1.0 / (1.0 + e)
    g2 = e / (1.0 + e)

    oh1 = jnp.where(lane == i1, 1.0, 0.0)
    oh2 = jnp.where(lane == i2, 1.0, 0.0)
    both = oh1 + oh2
    row = lax.broadcasted_iota(jnp.int32, (tm, tm), 0)
    col = lax.broadcasted_iota(jnp.int32, (tm, tm), 1)
    before = jnp.where(row > col, 1.0, 0.0).astype(BF16)
    seen = jnp.dot(before, both.astype(BF16), preferred_element_type=F32) + base_s[...]
    rank1 = jnp.sum(oh1 * seen, axis=-1, keepdims=True)
    rank2 = jnp.sum(oh2 * seen, axis=-1, keepdims=True)
    base = base_s[...] + jnp.sum(both, axis=0, keepdims=True)
    base_s[...] = base
    cnt_o[...] = jnp.broadcast_to(base, cnt_o.shape)

    meta = jnp.where(lane == 0.0, i1, jnp.where(lane == 1.0, i2, jnp.where(lane == 2.0, rank1, rank2)))
    meta_o[...] = meta.astype(jnp.int32)
    gate_o[...] = jnp.where(lane == 0.0, g1, g2)


def _router(h, gn, router):
    n, d = h.shape
    tm = ROUTER_ROWS
    rp = _pad_cols(router.astype(F32), LANES)
    rhi = rp.astype(BF16)
    rlo = (rp - rhi.astype(F32)).astype(BF16)
    rows = pl.BlockSpec((tm, d), lambda i: (i, 0))
    narrow = pl.BlockSpec((tm, LANES), lambda i: (i, 0))
    return pl.pallas_call(
        _router_kernel,
        grid=(n // tm,),
        in_specs=[rows, _full((1, d)), _full((d, LANES)), _full((d, LANES))],
        out_specs=[rows, narrow, narrow, _full((SUBLANES, LANES))],
        out_shape=[jax.ShapeDtypeStruct((n, d), F32),
                   jax.ShapeDtypeStruct((n, LANES), jnp.int32),
                   jax.ShapeDtypeStruct((n, LANES), F32),
                   jax.ShapeDtypeStruct((SUBLANES, LANES), F32)],
        scratch_shapes=[pltpu.VMEM((1, LANES), F32)],
        compiler_params=_params("arbitrary"),
        name="moe_router",
    )(h, gn.reshape(1, d).astype(F32), rhi, rlo)


def _row_copy(src_ref, src_row, dst_ref, dst_row, sem):
    return pltpu.make_async_copy(src_ref.at[pl.ds(src_row, 1)], dst_ref.at[pl.ds(dst_row, 1)], sem)


def _slot_maps_kernel(slot_ref, seg_ref, src_o, dst_o, *, n, tm):
    total = src_o.shape[0]
    n_tiles = total // tm - 1

    def fill(first, last, shift):
        def body(s, carry):
            src_o[s] = 0
            dst_o[tm + s] = s + shift
            return carry
        lax.fori_loop(first, last, body, 0)

    def spare_shift(tile):
        return 2 * n + (tile % 3) * tm - tile * tm

    for e in range(N_EXPERTS):
        fill(seg_ref[e], seg_ref[N_EXPERTS + e], seg_ref[2 * N_EXPERTS + e])

    def idle_tile(t, carry):
        fill(t * tm, (t + 1) * tm, spare_shift(t))
        return carry
    lax.fori_loop(seg_ref[3 * N_EXPERTS], n_tiles, idle_tile, 0)

    def lead(s, carry):
        dst_o[s] = s + spare_shift(2) + 2 * tm
        src_o[total - tm + s] = 0
        return carry
    lax.fori_loop(0, tm, lead, 0, unroll=8)

    def place(token, carry):
        for k in range(2):
            s = slot_ref[2 * token + k]
            src_o[s] = token
            dst_o[tm + s] = k * n + token
        return carry
    lax.fori_loop(0, n, place, 0, unroll=8)


def _slot_maps(slot, pad_start, cnt, pad_end, n, n_tiles):
    tm = MOE_ROWS
    total = (n_tiles + 1) * tm
    last_tile = (pad_end - 1) // tm
    shift = 2 * n + (last_tile % 3) * tm - last_tile * tm
    segments = jnp.concatenate([pad_start + cnt, pad_end, shift, pad_end[-1:] // tm]).astype(jnp.int32)
    smem = pl.BlockSpec(memory_space=pltpu.SMEM)
    return pl.pallas_call(
        functools.partial(_slot_maps_kernel, n=n, tm=tm),
        in_specs=[smem, smem],
        out_specs=[smem, smem],
        out_shape=[jax.ShapeDtypeStruct((total,), jnp.int32)] * 2,
        compiler_params=pltpu.CompilerParams(disable_bounds_checks=True),
        name="moe_slot_maps",
    )(slot, segments)


def _experts_kernel(te_ref, nu_ref, src_ref, dst_ref, xn_ref, wg_ref, wu_ref, wd_ref, out_ref,
                    xg_s, xb_s, acc_s, ob_s, gsem, osem, *, n_cols):
    del te_ref
    i = pl.program_id(0)
    f = pl.program_id(1)
    n_used = nu_ref[0]
    used = i < n_used
    tm = acc_s.shape[0]
    per_step = tm // n_cols
    slot = i % 2
    other = 1 - slot

    def gather_row(tile, buf, r):
        _row_copy(xn_ref, src_ref[tile * tm + r], xg_s.at[buf], r, gsem.at[buf]).start(priority=1)

    def scatter_row(tile, buf, r):
        _row_copy(ob_s.at[buf], r, out_ref, dst_ref[(tile + 1) * tm + r], osem.at[buf]).start(priority=0)

    def wait_gather(buf):
        pltpu.make_async_copy(xn_ref.at[pl.ds(0, tm)], xg_s.at[buf], gsem.at[buf]).wait()

    def wait_scatter(buf):
        pltpu.make_async_copy(ob_s.at[buf], out_ref.at[pl.ds(0, tm)], osem.at[buf]).wait()

    @pl.when(jnp.logical_and(used, f == 0))
    def _():
        @pl.when(i == 0)
        def _():
            def body(r, carry):
                gather_row(0, 0, r)
                return carry
            lax.fori_loop(0, tm, body, 0, unroll=8)
            ob_s[...] = jnp.zeros_like(ob_s)
            spare0 = out_ref.shape[0] - 3 * tm
            for k in range(3):
                pltpu.make_async_copy(ob_s.at[0], out_ref.at[pl.ds(spare0 + k * tm, tm)], osem.at[0]).start()
            for k in range(3):
                wait_scatter(0)

        wait_gather(slot)
        xb_s[...] = xg_s[slot].astype(BF16)
        acc_s[...] = jnp.zeros_like(acc_s)

    @pl.when(used)
    def _():
        for u in range(per_step):
            r = f * per_step + u
            gather_row(i + 1, other, r)
            scatter_row(i - 1, other, r)
        x = xb_s[...]
        gate = jnp.dot(x, wg_ref[0, 0].astype(BF16), preferred_element_type=F32)
        up = jnp.dot(x, wu_ref[0, 0].astype(BF16), preferred_element_type=F32)
        act = (gate * _sigmoid(gate) * up).astype(BF16)
        acc_s[...] += jnp.dot(act, wd_ref[0, 0].astype(BF16), preferred_element_type=F32)

    @pl.when(jnp.logical_and(used, f == n_cols - 1))
    def _():
        @pl.when(i > 0)
        def _():
            wait_scatter(slot)

        ob_s[slot] = acc_s[...]

        @pl.when(i == n_used - 1)
        def _():
            def body(r, carry):
                scatter_row(i, slot, r)
                return carry
            lax.fori_loop(0, tm, body, 0, unroll=8)
            wait_scatter(other)
            wait_scatter(slot)
            wait_gather(other)


def _experts(xn, tile_expert, n_used, src, dst, layer, w_gate, w_up, w_down):
    n, d = xn.shape
    fe = w_gate.shape[-1]
    tm, tf = MOE_ROWS, MOE_COLS
    nf = fe // tf
    assert tm % nf == 0
    n_tiles = tile_expert.shape[0]
    col = lambda i, f, nu: jnp.where(i < nu[0], f, nf - 1)
    return pl.pallas_call(
        functools.partial(_experts_kernel, n_cols=nf),
        grid_spec=pltpu.PrefetchScalarGridSpec(
            num_scalar_prefetch=4,
            grid=(n_tiles, nf),
            in_specs=[pl.BlockSpec(memory_space=pl.ANY),
                      pl.BlockSpec((1, 1, d, tf), lambda i, f, te, nu, s, t: (layer, te[i], 0, col(i, f, nu))),
                      pl.BlockSpec((1, 1, d, tf), lambda i, f, te, nu, s, t: (layer, te[i], 0, col(i, f, nu))),
                      pl.BlockSpec((1, 1, tf, d), lambda i, f, te, nu, s, t: (layer, te[i], col(i, f, nu), 0))],
            out_specs=pl.BlockSpec(memory_space=pl.ANY),
            scratch_shapes=[pltpu.VMEM((2, tm, d), F32), pltpu.VMEM((tm, d), BF16),
                            pltpu.VMEM((tm, d), F32), pltpu.VMEM((2, tm, d), F32),
                            pltpu.SemaphoreType.DMA((2,)), pltpu.SemaphoreType.DMA((2,))],
        ),
        out_shape=jax.ShapeDtypeStruct((2 * n + 3 * tm, d), F32),
        compiler_params=_params("arbitrary", "arbitrary"),
        name="moe_experts",
    )(tile_expert, n_used, src, dst, xn, w_gate, w_up, w_down)


def _combine_kernel(h_ref, gate_ref, fn_ref, e0_ref, e1_ref, o_ref, *, final_norm):
    gates = gate_ref[...]
    out = h_ref[...] + gates[:, 0:1] * e0_ref[...] + gates[:, 1:2] * e1_ref[...]
    if final_norm:
        out = _rms(out, fn_ref[...])
    o_ref[...] = out


def _combine(h, gates, eo, final_gain):
    n, d = h.shape
    tm = FFN_ROWS
    final_norm = final_gain is not None
    fn = (final_gain if final_norm else jnp.ones((d,), F32)).reshape(1, d).astype(F32)
    rows = pl.BlockSpec((tm, d), lambda i: (i, 0))
    return pl.pallas_call(
        functools.partial(_combine_kernel, final_norm=final_norm),
        grid=(n // tm,),
        in_specs=[rows, pl.BlockSpec((tm, LANES), lambda i: (i, 0)), _full((1, d)),
                  rows, pl.BlockSpec((tm, d), lambda i: (i + n // tm, 0))],
        out_specs=rows,
        out_shape=jax.ShapeDtypeStruct((n, d), F32),
        compiler_params=_params("arbitrary"),
        name="moe_combine",
    )(h, gates, fn, eo, eo)


def _moe(h, gn, router, layer, w_gate, w_up, w_down, final_gain):
    n, d = h.shape
    tm = MOE_ROWS
    xn, meta, gates, counts = _router(h, gn, router)
    n_tiles = -(-2 * n // tm) + N_EXPERTS
    cnt = counts[0, :N_EXPERTS].astype(jnp.int32)
    padded = (cnt + tm - 1) // tm * tm
    pad_end = jnp.cumsum(padded)
    pad_start = pad_end - padded
    slot = (pad_start[meta[:, 0:2]] + meta[:, 2:4]).reshape(-1).astype(jnp.int32)
    tile_start = jnp.arange(n_tiles, dtype=jnp.int32) * tm
    tile_expert = jnp.minimum(jnp.sum(tile_start[:, None] >= pad_end[None, :], axis=1),
                              N_EXPERTS - 1).astype(jnp.int32)
    n_used = (pad_end[-1:] // tm).astype(jnp.int32)
    src, dst = _slot_maps(slot, pad_start, cnt, pad_end, n, n_tiles)
    eo = _experts(xn, tile_expert, n_used, src, dst, layer, w_gate, w_up, w_down)
    return _combine(h, gates, eo, final_gain)


def kernel(x, norm_mix, norm_ffn, final_norm, rwkv_mu, rwkv_w_rkv, rwkv_w0, rwkv_w1, rwkv_w2, rwkv_a0, rwkv_a1, rwkv_a2, rwkv_g1, rwkv_g2, rwkv_k_k, rwkv_k_a, rwkv_r_k, rwkv_lnx_w, rwkv_lnx_b, rwkv_w_o, rwkv_v0, rwkv_v1, rwkv_v2, s5_lam_re, s5_lam_im, s5_log_step, s5_b_re, s5_b_im, s5_c_re, s5_c_im, s5_d, s5_w_glu, s5_b_glu, ffn_gate, ffn_up, ffn_down, moe_router, moe_gate, moe_up, moe_down):
    batch, seq, d = x.shape
    depth = norm_mix.shape[0]
    assert d % LANES == 0 and seq % WKV_CHUNK == 0
    n = batch * seq
    h = x.astype(F32).reshape(n, d)
    v_first = None
    for i in range(depth):
        j = i // 2
        last = i == depth - 1
        if i % 2 == 0:
            v_mix = None if j == 0 else (rwkv_v0[j - 1], rwkv_v1[j - 1], rwkv_v2[j - 1])
            r, lw, k, v, kk, b, g = _rwkv_proj(
                h, seq, norm_mix[i], rwkv_mu[j], rwkv_w_rkv[j], rwkv_w0[j], rwkv_w1[j], rwkv_w2[j],
                rwkv_a0[j], rwkv_a1[j], rwkv_a2[j], rwkv_g1[j], rwkv_g2[j], rwkv_k_k[j], rwkv_k_a[j],
                v_first, v_mix)
            if v_first is None:
                v_first = v
            yg = _wkv(r, lw, k, v, kk, b, g, rwkv_r_k[j], rwkv_lnx_w[j], rwkv_lnx_b[j], seq)
            h = _wo_ffn(h, yg, rwkv_w_o[j], norm_ffn[i], ffn_gate[j], ffn_up[j], ffn_down[j])
            if last:
                h = _final_norm(h, final_norm)
        else:
            h = _s5_glu(h, seq, norm_mix[i], s5_lam_re[j], s5_lam_im[j], s5_log_step[j], s5_b_re[j], s5_b_im[j],
                        s5_c_re[j], s5_c_im[j], s5_d[j], s5_w_glu[j], s5_b_glu[j])
            h = _moe(h, norm_ffn[i], moe_router[j], j, moe_gate, moe_up, moe_down,
                     final_norm if last else None)
    return h.reshape(batch, seq, d).astype(x.dtype)


def _final_norm_kernel(h_ref, g_ref, o_ref):
    o_ref[...] = _rms(h_ref[...], g_ref[...])


def _final_norm(h, gain):
    n, d = h.shape
    rows = pl.BlockSpec((FFN_ROWS, d), lambda i: (i, 0))
    return pl.pallas_call(
        _final_norm_kernel, grid=(n // FFN_ROWS,), in_specs=[rows, _full((1, d))], out_specs=rows,
        out_shape=jax.ShapeDtypeStruct((n, d), F32), compiler_params=_params("arbitrary"),
        name="final_norm",
    )(h, gain.reshape(1, d).astype(F32))
```
